```python
import jax, jax.numpy as jnp
from jax import lax
import numpy as np

D_MODEL = 1024
BATCH = 8
SEQ = 2048
DEPTH = 4

MIX_WIDTH = 2 * D_MODEL
CHUNK = 128
GMLP_WIDTH = MIX_WIDTH // 2
GMLP_HEADS = 8
GMLP_HEAD_DIM = GMLP_WIDTH // GMLP_HEADS
SSD_WIDTH = MIX_WIDTH - GMLP_WIDTH
SSD_HEAD_DIM = 64
SSD_HEADS = SSD_WIDTH // SSD_HEAD_DIM
SSD_GROUPS = 4
HEADS_PER_GROUP = SSD_HEADS // SSD_GROUPS
D_STATE = 128
CONV_WIDTH = 4
CONV_DIM = SSD_WIDTH + 2 * SSD_GROUPS * D_STATE
IN_COLS = 3 * GMLP_WIDTH + SSD_WIDTH + CONV_DIM + SSD_HEADS
SPLITS = (GMLP_WIDTH, 2 * GMLP_WIDTH, 3 * GMLP_WIDTH,
          3 * GMLP_WIDTH + SSD_WIDTH, 3 * GMLP_WIDTH + SSD_WIDTH + CONV_DIM)
EPS = 1e-6
DT_MIN = 1e-3
DT_MAX = 1e-1

kernel_name = "hymba_gmlp_ssd_hybrid"


def rms_norm(x, w):
    xf = x.astype(jnp.float32)
    y = xf * lax.rsqrt(jnp.mean(xf * xf, axis=-1, keepdims=True) + EPS)
    return (y * w.astype(jnp.float32)).astype(x.dtype)


def grouped_rms_norm(x, w, group_size):
    shp = x.shape
    xf = x.astype(jnp.float32).reshape(*shp[:-1], shp[-1] // group_size, group_size)
    y = xf * lax.rsqrt(jnp.mean(xf * xf, axis=-1, keepdims=True) + EPS)
    return (y.reshape(shp) * w.astype(jnp.float32)).astype(x.dtype)


def layer_norm(x, w, b):
    xf = x.astype(jnp.float32)
    mu = jnp.mean(xf, axis=-1, keepdims=True)
    xc = xf - mu
    y = xc * lax.rsqrt(jnp.mean(xc * xc, axis=-1, keepdims=True) + EPS)
    return (y * w.astype(jnp.float32) + b.astype(jnp.float32)).astype(x.dtype)


def causal_depthwise_conv(x, w, b):
    k, c = w.shape
    out = lax.conv_general_dilated(
        x, w[:, None, :].astype(x.dtype), window_strides=(1,), padding=[(k - 1, 0)],
        dimension_numbers=('NWC', 'WIO', 'NWC'), feature_group_count=c)
    return out + b.astype(x.dtype)


def spatial_gating(u, v, zg, v_norm_w, v_norm_b, ws, bs, out_norm_w):
    bsz, seq, _ = u.shape
    nc = seq // CHUNK
    vn = layer_norm(v, v_norm_w, v_norm_b).reshape(bsz, nc, CHUNK, GMLP_HEADS, GMLP_HEAD_DIM)
    causal = jnp.tril(jnp.ones((CHUNK, CHUNK), dtype=bool))
    ws_c = jnp.where(causal[None], ws, 0.0)
    s = jnp.einsum('hts,bcshd->bcthd', ws_c.astype(vn.dtype), vn) + bs.T[None, None, :, :, None]
    s = s.reshape(bsz, seq, GMLP_WIDTH)
    y = u * s * jax.nn.silu(zg)
    return grouped_rms_norm(y, out_norm_w, GMLP_HEAD_DIM)


def ssd_mixer(xbc, z, dt_raw, conv_w, conv_b, dt_bias, a_log, d_skip, norm_w):
    bsz, seq, _ = xbc.shape
    nc = seq // CHUNK
    f32 = jnp.float32
    xbc = jax.nn.silu(causal_depthwise_conv(xbc, conv_w, conv_b))
    xs, bm, cm = jnp.split(xbc, [SSD_WIDTH, SSD_WIDTH + SSD_GROUPS * D_STATE], axis=-1)
    xs = xs.astype(f32).reshape(bsz, nc, CHUNK, SSD_GROUPS, HEADS_PER_GROUP, SSD_HEAD_DIM)
    bm = bm.astype(f32).reshape(bsz, nc, CHUNK, SSD_GROUPS, D_STATE)
    cm = cm.astype(f32).reshape(bsz, nc, CHUNK, SSD_GROUPS, D_STATE)
    dt = jax.nn.softplus(dt_raw.astype(f32) + dt_bias.astype(f32))
    dt = dt.reshape(bsz, nc, CHUNK, SSD_GROUPS, HEADS_PER_GROUP)
    a = -jnp.exp(a_log.astype(f32)).reshape(SSD_GROUPS, HEADS_PER_GROUP)
    da_cs = jnp.cumsum(dt * a, axis=2)
    xdt = xs * dt[..., None]
    causal = jnp.tril(jnp.ones((CHUNK, CHUNK), dtype=bool))[None, None, :, :, None, None]
    seg = da_cs[:, :, :, None] - da_cs[:, :, None, :]
    decay = jnp.exp(jnp.where(causal, seg, -jnp.inf))
    cb = jnp.einsum('bclgn,bcsgn->bclsg', cm, bm)
    y_diag = jnp.einsum('bclsg,bclsgr,bcsgrp->bclgrp', cb, decay, xdt)
    decay_to_end = jnp.exp(da_cs[:, :, -1:] - da_cs)
    states = jnp.einsum('bclgn,bclgr,bclgrp->bcgrpn', bm, decay_to_end, xdt)
    chunk_decay = jnp.exp(da_cs[:, :, -1])

    def step(carry, inp):
        st, dec = inp
        return carry * dec[..., None, None] + st, carry

    init = jnp.zeros_like(states[:, 0])
    _, prev_states = lax.scan(step, init, (jnp.moveaxis(states, 1, 0), jnp.moveaxis(chunk_decay, 1, 0)))
    prev_states = jnp.moveaxis(prev_states, 0, 1)
    y_off = jnp.einsum('bclgn,bcgrpn,bclgr->bclgrp', cm, prev_states, jnp.exp(da_cs))
    d = d_skip.astype(f32).reshape(SSD_GROUPS, HEADS_PER_GROUP)[:, :, None]
    y = (y_diag + y_off + xs * d).reshape(bsz, seq, SSD_WIDTH).astype(z.dtype)
    return grouped_rms_norm(y * jax.nn.silu(z), norm_w, SSD_WIDTH // SSD_GROUPS)


def hybrid_layer(x, pre_w, w_in, v_norm_w, v_norm_b, ws, bs, gmlp_norm_w,
                 conv_w, conv_b, dt_bias, a_log, d_skip, ssd_norm_w, w_out, post_w):
    h = rms_norm(x, pre_w)
    proj = jnp.einsum('bld,dc->blc', h, w_in)
    u, v, zg, z, xbc, dt_raw = jnp.split(proj, SPLITS, axis=-1)
    y_a = spatial_gating(u, v, zg, v_norm_w, v_norm_b, ws, bs, gmlp_norm_w)
    y_b = ssd_mixer(xbc, z, dt_raw, conv_w, conv_b, dt_bias, a_log, d_skip, ssd_norm_w)
    mixed = jnp.einsum('blc,cd->bld', jnp.concatenate([y_a, y_b], axis=-1), w_out)
    return x + rms_norm(mixed, post_w)


def setup_inputs(seed: int = 0) -> dict:
    key = jax.random.key(seed)
    ks = jax.random.split(key, 18)
    nrm = jax.random.normal
    dt = jnp.exp(jax.random.uniform(ks[10], (DEPTH, SSD_HEADS)) * (np.log(DT_MAX) - np.log(DT_MIN)) + np.log(DT_MIN))
    return {
        "x": nrm(ks[0], (BATCH, SEQ, D_MODEL), jnp.float32),
        "pre_norm_w": 1.0 + 0.02 * nrm(ks[1], (DEPTH, D_MODEL)),
        "w_in": nrm(ks[2], (DEPTH, D_MODEL, IN_COLS)) * D_MODEL ** -0.5,
        "gmlp_v_norm_w": 1.0 + 0.02 * nrm(ks[3], (DEPTH, GMLP_WIDTH)),
        "gmlp_v_norm_b": 0.02 * nrm(ks[4], (DEPTH, GMLP_WIDTH)),
        "gmlp_ws": nrm(ks[5], (DEPTH, GMLP_HEADS, CHUNK, CHUNK)) * CHUNK ** -0.5,
        "gmlp_bs": 1.0 + 0.02 * nrm(ks[6], (DEPTH, GMLP_HEADS, CHUNK)),
        "gmlp_norm_w": 1.0 + 0.02 * nrm(ks[7], (DEPTH, GMLP_WIDTH)),
        "conv_w": nrm(ks[8], (DEPTH, CONV_WIDTH, CONV_DIM)) * CONV_WIDTH ** -0.5,
        "conv_b": 0.02 * nrm(ks[9], (DEPTH, CONV_DIM)),
        "dt_bias": dt + jnp.log(-jnp.expm1(-dt)),
        "a_log": jnp.log(jax.random.uniform(ks[11], (DEPTH, SSD_HEADS), minval=1.0, maxval=16.0)),
        "d_skip": 1.0 + 0.02 * nrm(ks[12], (DEPTH, SSD_HEADS)),
        "ssd_norm_w": 1.0 + 0.02 * nrm(ks[13], (DEPTH, SSD_WIDTH)),
        "w_out": nrm(ks[14], (DEPTH, MIX_WIDTH, D_MODEL)) * MIX_WIDTH ** -0.5,
        "post_norm_w": 1.0 + 0.02 * nrm(ks[15], (DEPTH, D_MODEL)),
    }


def reference(x, pre_norm_w, w_in, gmlp_v_norm_w, gmlp_v_norm_b, gmlp_ws, gmlp_bs, gmlp_norm_w,
              conv_w, conv_b, dt_bias, a_log, d_skip, ssd_norm_w, w_out, post_norm_w):
    h = x
    for i in range(DEPTH):
        h = hybrid_layer(h, pre_norm_w[i], w_in[i], gmlp_v_norm_w[i], gmlp_v_norm_b[i],
                         gmlp_ws[i], gmlp_bs[i], gmlp_norm_w[i], conv_w[i], conv_b[i],
                         dt_bias[i], a_log[i], d_skip[i], ssd_norm_w[i], w_out[i], post_norm_w[i])
    return h
```

```python
import functools

import jax
import jax.numpy as jnp
from jax import lax
from jax.experimental import pallas as pl
from jax.experimental.pallas import tpu as pltpu

D_MODEL = 1024
CHUNK = 128
GMLP_WIDTH = 1024
GMLP_HEADS = 8
GMLP_HEAD_DIM = 128
SSD_WIDTH = 1024
SSD_HEAD_DIM = 64
SSD_HEADS = 16
SSD_GROUPS = 4
HEADS_PER_GROUP = 4
GROUP_WIDTH = SSD_WIDTH // SSD_GROUPS
D_STATE = 128
CONV_WIDTH = 4
CONV_DIM = SSD_WIDTH + 2 * SSD_GROUPS * D_STATE
MIX_WIDTH = GMLP_WIDTH + SSD_WIDTH
EPS = 1e-6

OFF_U = 0
OFF_V = GMLP_WIDTH
OFF_ZG = 2 * GMLP_WIDTH
OFF_Z = 3 * GMLP_WIDTH
OFF_XBC = 3 * GMLP_WIDTH + SSD_WIDTH
MAIN_COLS = OFF_XBC + CONV_DIM
OFF_B = SSD_WIDTH
OFF_C = SSD_WIDTH + SSD_GROUPS * D_STATE

LANES = 128
SUBLANES = 8
PAD_ROWS = SUBLANES
ROW_TILE = 256
PROJ_COL_BLOCK = 1024

F32 = jnp.float32
BF16 = jnp.bfloat16


def _split3(x):
    hi = x.astype(BF16)
    r1 = x - hi.astype(F32)
    mid = r1.astype(BF16)
    lo = (r1 - mid.astype(F32)).astype(BF16)
    return hi, mid, lo


def _exact_dot_rhs(mat01, x):
    hi, mid, lo = _split3(x)
    d = functools.partial(jnp.dot, preferred_element_type=F32)
    return d(mat01, hi) + d(mat01, mid) + d(mat01, lo)


def _exact_dot_lhs(x, mat01):
    hi, mid, lo = _split3(x)
    d = functools.partial(jnp.dot, preferred_element_type=F32)
    return d(hi, mat01) + d(mid, mat01) + d(lo, mat01)


def _layer_kernel(x_ref, pre_w_ref, w_main_ref, w_dt_ref, vw_ref, vb_ref, ws_ref, bsf_ref,
                  gw_ref, cw_ref, cb_ref, dtb_ref, alog_ref, dsk_ref, nw_ref, w_out_ref,
                  post_w_ref, out_ref,
                  proj_ref, dtraw_ref, act_ref, ycat_ref, state_ref, wsm_ref, expand_ref,
                  *, row_tile):
    n_chunks = row_tile // CHUNK
    t = pl.program_id(1)

    row_i = lax.broadcasted_iota(jnp.int32, (CHUNK, CHUNK), 0)
    col_i = lax.broadcasted_iota(jnp.int32, (CHUNK, CHUNK), 1)
    causal = row_i >= col_i

    @pl.when(t == 0)
    def _start_of_sequence():
        state_ref[...] = jnp.zeros_like(state_ref)
        proj_ref[0:PAD_ROWS, OFF_XBC:MAIN_COLS] = jnp.zeros((PAD_ROWS, CONV_DIM), F32)
        for hd in range(GMLP_HEADS):
            wsm_ref[hd] = jnp.where(causal, ws_ref[hd], 0.0).astype(BF16)
        eh = lax.broadcasted_iota(jnp.int32, (LANES, SSD_WIDTH), 0)
        ej = lax.broadcasted_iota(jnp.int32, (LANES, SSD_WIDTH), 1)
        expand_ref[...] = jnp.where(ej // SSD_HEAD_DIM == eh, 1.0, 0.0).astype(BF16)

    x = x_ref[0]
    h = x * lax.rsqrt(jnp.mean(x * x, axis=-1, keepdims=True) + EPS) * pre_w_ref[...]
    hb = h.astype(BF16)
    for j in range(MAIN_COLS // PROJ_COL_BLOCK):
        cols = slice(j * PROJ_COL_BLOCK, (j + 1) * PROJ_COL_BLOCK)
        proj_ref[PAD_ROWS:PAD_ROWS + row_tile, cols] = jnp.dot(
            hb, w_main_ref[:, cols], preferred_element_type=F32)
    dtraw_ref[...] = jnp.dot(hb, w_dt_ref[...], preferred_element_type=F32)

    tril_bf = jnp.where(causal, 1.0, 0.0).astype(BF16)
    lane_g = lax.broadcasted_iota(jnp.int32, (CHUNK, GROUP_WIDTH), 1) // SSD_HEAD_DIM
    a_row = -jnp.exp(alog_ref[...])

    for c in range(n_chunks):
        r0 = PAD_ROWS + c * CHUNK
        rows = slice(r0, r0 + CHUNK)
        orow = slice(c * CHUNK, (c + 1) * CHUNK)

        v = proj_ref[rows, OFF_V:OFF_V + GMLP_WIDTH]
        mu = jnp.mean(v, axis=-1, keepdims=True)
        vc = v - mu
        vn = vc * lax.rsqrt(jnp.mean(vc * vc, axis=-1, keepdims=True) + EPS)
        vnb = (vn * vw_ref[...] + vb_ref[...]).astype(BF16)
        for hd in range(GMLP_HEADS):
            cs_ = slice(hd * GMLP_HEAD_DIM, (hd + 1) * GMLP_HEAD_DIM)
            s = jnp.dot(wsm_ref[hd], vnb[:, cs_], preferred_element_type=F32) + bsf_ref[:, cs_]
            u = proj_ref[rows, OFF_U + hd * GMLP_HEAD_DIM:OFF_U + (hd + 1) * GMLP_HEAD_DIM]
            zg = proj_ref[rows, OFF_ZG + hd * GMLP_HEAD_DIM:OFF_ZG + (hd + 1) * GMLP_HEAD_DIM]
            y = u * s * jax.nn.silu(zg)
            yn = y * lax.rsqrt(jnp.mean(y * y, axis=-1, keepdims=True) + EPS) * gw_ref[:, cs_]
            ycat_ref[orow, cs_] = yn.astype(BF16)

        for j in range(CONV_DIM // GROUP_WIDTH):
            cc = slice(j * GROUP_WIDTH, (j + 1) * GROUP_WIDTH)
            pc = slice(OFF_XBC + j * GROUP_WIDTH, OFF_XBC + (j + 1) * GROUP_WIDTH)
            acc = cb_ref[:, cc] + cw_ref[3:4, cc] * proj_ref[r0:r0 + CHUNK, pc]
            acc = acc + cw_ref[2:3, cc] * proj_ref[r0 - 1:r0 - 1 + CHUNK, pc]
            acc = acc + cw_ref[1:2, cc] * proj_ref[r0 - 2:r0 - 2 + CHUNK, pc]
            acc = acc + cw_ref[0:1, cc] * proj_ref[r0 - 3:r0 - 3 + CHUNK, pc]
            act_ref[:, cc] = jax.nn.silu(acc)

        dt = jax.nn.softplus(dtraw_ref[orow, :] + dtb_ref[...])
        da = dt * a_row
        cs = _exact_dot_rhs(tril_bf, da)
        cs_t = cs.T
        cs_last = cs[CHUNK - 1:CHUNK, :]
        ecs = jnp.exp(cs)
        dte = jnp.exp(cs_last - cs)
        cdec = jnp.exp(cs_last)
        dt_x = _exact_dot_lhs(dt, expand_ref[...])
        dd_x = _exact_dot_lhs(dt * dte, expand_ref[...])
        ecs_x = _exact_dot_lhs(ecs, expand_ref[...])
        cdec_x = _exact_dot_lhs(jnp.broadcast_to(cdec, (SUBLANES, LANES)), expand_ref[...])[0:1, :]

        for g in range(SSD_GROUPS):
            gc = slice(g * GROUP_WIDTH, (g + 1) * GROUP_WIDTH)
            xs_g = act_ref[:, gc]
            b_g = act_ref[:, OFF_B + g * D_STATE:OFF_B + (g + 1) * D_STATE].astype(BF16)
            c_g = act_ref[:, OFF_C + g * D_STATE:OFF_C + (g + 1) * D_STATE].astype(BF16)
            cb_g = lax.dot_general(c_g, b_g, (((1,), (1,)), ((), ())), preferred_element_type=F32)
            xdt_g = (xs_g * dt_x[:, gc]).astype(BF16)
            g_blocks = []
            x_blocks = []
            for r in range(HEADS_PER_GROUP):
                hh = g * HEADS_PER_GROUP + r
                seg = cs[:, hh:hh + 1] - cs_t[hh:hh + 1, :]
                dec = jnp.exp(jnp.where(causal, seg, -jnp.inf))
                g_blocks.append((cb_g * dec).astype(BF16))
                x_blocks.append(jnp.where(lane_g == r, xdt_g, jnp.zeros_like(xdt_g)))
            y_diag = jnp.dot(jnp.concatenate(g_blocks, axis=1), jnp.concatenate(x_blocks, axis=0),
                             preferred_element_type=F32)
            st = state_ref[g]
            y_off = jnp.dot(c_g, st.astype(BF16), preferred_element_type=F32) * ecs_x[:, gc]
            xdd_g = (xs_g * dd_x[:, gc]).astype(BF16)
            st_new = lax.dot_general(b_g, xdd_g, (((0,), (0,)), ((), ())), preferred_element_type=F32)
            state_ref[g] = st * cdec_x[:, gc] + st_new
            y = y_diag + y_off + xs_g * dsk_ref[:, gc]
            z = proj_ref[rows, OFF_Z + g * GROUP_WIDTH:OFF_Z + (g + 1) * GROUP_WIDTH]
            yz = y * jax.nn.silu(z)
            yn = yz * lax.rsqrt(jnp.mean(yz * yz, axis=-1, keepdims=True) + EPS) * nw_ref[:, gc]
            ycat_ref[orow, GMLP_WIDTH + g * GROUP_WIDTH:GMLP_WIDTH + (g + 1) * GROUP_WIDTH] = yn.astype(BF16)

    proj_ref[0:PAD_ROWS, OFF_XBC:MAIN_COLS] = proj_ref[row_tile:row_tile + PAD_ROWS, OFF_XBC:MAIN_COLS]

    mixed = jnp.dot(ycat_ref[...], w_out_ref[...], preferred_element_type=F32)
    mn = mixed * lax.rsqrt(jnp.mean(mixed * mixed, axis=-1, keepdims=True) + EPS) * post_w_ref[...]
    out_ref[0] = x_ref[0] + mn


def _vmem_limit_bytes(row_tile):
    weights = (D_MODEL * MAIN_COLS + D_MODEL * LANES + MIX_WIDTH * D_MODEL) * 2
    small = (GMLP_HEADS * CHUNK * CHUNK + CHUNK * GMLP_WIDTH) * 4 + 64 * 1024
    io = 2 * 2 * row_tile * D_MODEL * 4
    scratch = ((PAD_ROWS + row_tile) * MAIN_COLS * 4 + row_tile * LANES * 4 + CHUNK * CONV_DIM * 4
               + row_tile * MIX_WIDTH * 2 + SSD_GROUPS * D_STATE * GROUP_WIDTH * 4
               + GMLP_HEADS * CHUNK * CHUNK * 2 + LANES * SSD_WIDTH * 2)
    temporaries = 12 * 1024 * 1024
    return 2 * weights + 2 * small + io + scratch + temporaries


def _layer(x, pre_w, w_main, w_dt, vw, vb, ws, bsf, gw, cw, cb, dtb, alog, dsk, nw, w_out, post_w,
           *, row_tile=ROW_TILE):
    bsz, seq, _ = x.shape
    assert seq % row_tile == 0 and row_tile % CHUNK == 0
    grid = (bsz, seq // row_tile)

    def const(shape):
        nd = len(shape)
        return pl.BlockSpec(shape, lambda b, t: (0,) * nd)

    x_spec = pl.BlockSpec((1, row_tile, D_MODEL), lambda b, t: (b, t, 0))
    in_specs = [
        x_spec,
        const((1, D_MODEL)),
        const((D_MODEL, MAIN_COLS)),
        const((D_MODEL, LANES)),
        const((1, GMLP_WIDTH)),
        const((1, GMLP_WIDTH)),
        const((GMLP_HEADS, CHUNK, CHUNK)),
        const((CHUNK, GMLP_WIDTH)),
        const((1, GMLP_WIDTH)),
        const((CONV_WIDTH, CONV_DIM)),
        const((1, CONV_DIM)),
        const((1, LANES)),
        const((1, LANES)),
        const((1, SSD_WIDTH)),
        const((1, SSD_WIDTH)),
        const((MIX_WIDTH, D_MODEL)),
        const((1, D_MODEL)),
    ]
    scratch = [
        pltpu.VMEM((PAD_ROWS + row_tile, MAIN_COLS), F32),
        pltpu.VMEM((row_tile, LANES), F32),
        pltpu.VMEM((CHUNK, CONV_DIM), F32),
        pltpu.VMEM((row_tile, MIX_WIDTH), BF16),
        pltpu.VMEM((SSD_GROUPS, D_STATE, GROUP_WIDTH), F32),
        pltpu.VMEM((GMLP_HEADS, CHUNK, CHUNK), BF16),
        pltpu.VMEM((LANES, SSD_WIDTH), BF16),
    ]
    return pl.pallas_call(
        functools.partial(_layer_kernel, row_tile=row_tile),
        grid=grid,
        in_specs=in_specs,
        out_specs=x_spec,
        out_shape=jax.ShapeDtypeStruct(x.shape, x.dtype),
        scratch_shapes=scratch,
        compiler_params=pltpu.CompilerParams(
            dimension_semantics=("arbitrary", "arbitrary"),
            vmem_limit_bytes=_vmem_limit_bytes(row_tile)),
        name="hybrid_layer",
    )(x, pre_w, w_main, w_dt, vw, vb, ws, bsf, gw, cw, cb, dtb, alog, dsk, nw, w_out, post_w)


def kernel(x, pre_norm_w, w_in, gmlp_v_norm_w, gmlp_v_norm_b, gmlp_ws, gmlp_bs, gmlp_norm_w,
           conv_w, conv_b, dt_bias, a_log, d_skip, ssd_norm_w, w_out, post_norm_w):
    depth = w_in.shape[0]
    pad_heads = LANES - SSD_HEADS
    h = x
    for i in range(depth):
        w_main = w_in[i, :, :MAIN_COLS].astype(BF16)
        w_dt = jnp.pad(w_in[i, :, MAIN_COLS:], ((0, 0), (0, pad_heads))).astype(BF16)
        bsf = jnp.repeat(gmlp_bs[i].T, GMLP_HEAD_DIM, axis=1)
        h = _layer(
            h,
            pre_norm_w[i][None, :],
            w_main,
            w_dt,
            gmlp_v_norm_w[i][None, :],
            gmlp_v_norm_b[i][None, :],
            gmlp_ws[i],
            bsf,
            gmlp_norm_w[i][None, :],
            conv_w[i],
            conv_b[i][None, :],
            jnp.pad(dt_bias[i], (0, pad_heads))[None, :],
            jnp.pad(a_log[i], (0, pad_heads))[None, :],
            jnp.repeat(d_skip[i], SSD_HEAD_DIM)[None, :],
            ssd_norm_w[i][None, :],
            w_out[i].astype(BF16),
            post_norm_w[i][None, :],
        )
    return h
```

```python
import functools

import jax
import jax.numpy as jnp
from jax import lax
from jax.experimental import pallas as pl
from jax.experimental.pallas import tpu as pltpu

D_MODEL = 1024
CHUNK = 128
GMLP_WIDTH = 1024
GMLP_HEADS = 8
GMLP_HEAD_DIM = 128
SSD_WIDTH = 1024
SSD_HEAD_DIM = 64
SSD_HEADS = 16
SSD_GROUPS = 4
HEADS_PER_GROUP = 4
GROUP_WIDTH = SSD_WIDTH // SSD_GROUPS
D_STATE = 128
CONV_WIDTH = 4
CONV_DIM = SSD_WIDTH + 2 * SSD_GROUPS * D_STATE
MIX_WIDTH = GMLP_WIDTH + SSD_WIDTH
EPS = 1e-6

OFF_U = 0
OFF_V = GMLP_WIDTH
OFF_ZG = 2 * GMLP_WIDTH
OFF_Z = 3 * GMLP_WIDTH
OFF_XBC = 3 * GMLP_WIDTH + SSD_WIDTH
MAIN_COLS = OFF_XBC + CONV_DIM
OFF_B = SSD_WIDTH
OFF_C = SSD_WIDTH + SSD_GROUPS * D_STATE

LANES = 128
SUBLANES = 8
MXU_DIM = 256
HEAD_REPLICAS = LANES // SSD_HEADS
PAD_ROWS = SUBLANES
ROW_TILE = 256
PROJ_COL_BLOCK = MXU_DIM
OUT_K_BLOCK = 2 * MXU_DIM
N_SPLIT = 3

F32 = jnp.float32
BF16 = jnp.bfloat16


def _split3(x):
    hi = x.astype(BF16).astype(F32)
    r1 = x - hi
    mid = r1.astype(BF16).astype(F32)
    lo = r1 - mid
    return hi, mid, lo


def _layer_kernel(x_ref, pre_w_ref, w_main_ref, w_dt_ref, vw_ref, vb_ref, ws_ref, bsf_ref,
                  gw_ref, cw_ref, cb_ref, dtb_ref, alog_ref, dsk_ref, nw_ref, w_out_ref,
                  post_w_ref, out_ref,
                  proj0_ref, proj1_ref, xbc0_ref, xbc1_ref, dtraw0_ref, dtraw1_ref, xkeep0_ref, xkeep1_ref,
                  act_ref, ycat_ref, state_ref, wsm_ref, expand_ref, mixed_ref,
                  *, row_tile, tiles_per_seq):
    n_chunks = row_tile // CHUNK
    s = pl.program_id(0)
    t_prev = (s + tiles_per_seq - 1) % tiles_per_seq

    row_i = lax.broadcasted_iota(jnp.int32, (CHUNK, CHUNK), 0)
    col_i = lax.broadcasted_iota(jnp.int32, (CHUNK, CHUNK), 1)
    causal = row_i >= col_i

    @pl.when(s == 0)
    def _first_step():
        proj1_ref[...] = jnp.zeros_like(proj1_ref)
        xbc1_ref[...] = jnp.zeros_like(xbc1_ref)
        dtraw1_ref[...] = jnp.zeros_like(dtraw1_ref)
        xkeep1_ref[...] = jnp.zeros_like(xkeep1_ref)
        state_ref[...] = jnp.zeros_like(state_ref)
        for hd in range(GMLP_HEADS):
            wsm_ref[hd] = jnp.where(causal, ws_ref[hd], 0.0).astype(BF16)
        ei = lax.broadcasted_iota(jnp.int32, (LANES, 2 * SSD_WIDTH), 0)
        ej = lax.broadcasted_iota(jnp.int32, (LANES, 2 * SSD_WIDTH), 1)
        which = ej // SSD_WIDTH
        head = (ej % SSD_WIDTH) // SSD_HEAD_DIM
        grp = ei // SSD_HEADS
        hit = (ei % SSD_HEADS == head) & (grp >= which * N_SPLIT) & (grp < (which + 1) * N_SPLIT)
        expand_ref[...] = jnp.where(hit, 1.0, 0.0).astype(BF16)

    def run(proj_a_ref, xbc_a_ref, dtraw_a_ref, xkeep_a_ref, proj_b_ref, xbc_b_ref, dtraw_b_ref, xkeep_b_ref):
        @pl.when(t_prev == 0)
        def _start_of_sequence():
            state_ref[...] = jnp.zeros_like(state_ref)
            xbc_b_ref[:, 0:PAD_ROWS, :] = jnp.zeros((CONV_DIM // LANES, PAD_ROWS, LANES), F32)

        x = x_ref[0]
        xkeep_a_ref[...] = x
        h = x * lax.rsqrt(jnp.mean(x * x, axis=-1, keepdims=True) + EPS) * pre_w_ref[...]
        hb = h.astype(BF16)

        def proj_task(j):
            cols = slice(j * PROJ_COL_BLOCK, (j + 1) * PROJ_COL_BLOCK)
            d = jnp.dot(hb, w_main_ref[:, cols], preferred_element_type=F32)
            if cols.start < OFF_XBC:
                proj_a_ref[:, cols] = d
            else:
                for i in range(PROJ_COL_BLOCK // LANES):
                    blk = (cols.start - OFF_XBC) // LANES + i
                    xbc_a_ref[blk, PAD_ROWS:PAD_ROWS + row_tile, :] = d[:, i * LANES:(i + 1) * LANES]

        def dt_task():
            dtraw_a_ref[...] = jnp.dot(hb, w_dt_ref[...], preferred_element_type=F32)

        a_tasks = [dt_task] + [functools.partial(proj_task, j) for j in range(MAIN_COLS // PROJ_COL_BLOCK)]

        def emit_a(n=1):
            for _ in range(n):
                if a_tasks:
                    a_tasks.pop(0)()

        def out_task(kb):
            ks = slice(kb * OUT_K_BLOCK, (kb + 1) * OUT_K_BLOCK)
            d = jnp.dot(ycat_ref[:, ks], w_out_ref[ks, :], preferred_element_type=F32)
            if kb == 0:
                mixed_ref[...] = d
            else:
                mixed_ref[...] += d

        tril_bf = jnp.where(causal, 1.0, 0.0).astype(BF16)
        lane_head_in_group = lax.broadcasted_iota(jnp.int32, (CHUNK, GROUP_WIDTH), 1) // SSD_HEAD_DIM
        lane_piece = col_i // SSD_HEADS
        a_row = -jnp.exp(alog_ref[...])

        emit_a(2)
        vnb = []
        for c in range(n_chunks):
            rows = slice(c * CHUNK, (c + 1) * CHUNK)
            v = proj_b_ref[rows, OFF_V:OFF_V + GMLP_WIDTH]
            mu = jnp.mean(v, axis=-1, keepdims=True)
            vc = v - mu
            vn = vc * lax.rsqrt(jnp.mean(vc * vc, axis=-1, keepdims=True) + EPS)
            vnb.append((vn * vw_ref[...] + vb_ref[...]).astype(BF16))
            emit_a()
        heads_per_block = OUT_K_BLOCK // GMLP_HEAD_DIM
        for kb in range(GMLP_WIDTH // OUT_K_BLOCK):
            for c in range(n_chunks):
                rows = slice(c * CHUNK, (c + 1) * CHUNK)
                for hd in range(kb * heads_per_block, (kb + 1) * heads_per_block):
                    cs_ = slice(hd * GMLP_HEAD_DIM, (hd + 1) * GMLP_HEAD_DIM)
                    sg = jnp.dot(wsm_ref[hd], vnb[c][:, cs_], preferred_element_type=F32) + bsf_ref[:, cs_]
                    u = proj_b_ref[rows, OFF_U + hd * GMLP_HEAD_DIM:OFF_U + (hd + 1) * GMLP_HEAD_DIM]
                    zg = proj_b_ref[rows, OFF_ZG + hd * GMLP_HEAD_DIM:OFF_ZG + (hd + 1) * GMLP_HEAD_DIM]
                    y = u * sg * jax.nn.silu(zg)
                    yn = y * lax.rsqrt(jnp.mean(y * y, axis=-1, keepdims=True) + EPS) * gw_ref[:, cs_]
                    ycat_ref[rows, cs_] = yn.astype(BF16)
            out_task(kb)
            emit_a()

        groups_per_block = OUT_K_BLOCK // GROUP_WIDTH
        for c in range(n_chunks):
            rows = slice(c * CHUNK, (c + 1) * CHUNK)
            r0 = PAD_ROWS + c * CHUNK

            for j in range(CONV_DIM // LANES):
                cc = slice(j * LANES, (j + 1) * LANES)
                acc = cb_ref[:, cc] + cw_ref[3:4, cc] * xbc_b_ref[j, r0:r0 + CHUNK, :]
                acc = acc + cw_ref[2:3, cc] * xbc_b_ref[j, r0 - 1:r0 - 1 + CHUNK, :]
                acc = acc + cw_ref[1:2, cc] * xbc_b_ref[j, r0 - 2:r0 - 2 + CHUNK, :]
                acc = acc + cw_ref[0:1, cc] * xbc_b_ref[j, r0 - 3:r0 - 3 + CHUNK, :]
                act_ref[:, cc] = jax.nn.silu(acc)
                if j % 4 == 3:
                    emit_a()

            dt = jax.nn.softplus(dtraw_b_ref[rows, :] + dtb_ref[...])
            da = dt * a_row
            da_pieces = jnp.concatenate(_split3(da), axis=1).astype(BF16)
            cs3 = jnp.dot(tril_bf, da_pieces, preferred_element_type=F32)
            cs = cs3[:, 0:LANES] + cs3[:, LANES:2 * LANES] + cs3[:, 2 * LANES:3 * LANES]
            cs_t = cs.T
            dt_t = dt.T
            cs_last = cs[CHUNK - 1:CHUNK, :]
            ecs = jnp.exp(cs)
            dd = dt * jnp.exp(cs_last - cs)
            pieces = _split3(dd) + _split3(ecs)
            packed = jnp.zeros((CHUNK, LANES), F32)
            for k, piece in enumerate(pieces):
                packed = jnp.where(lane_piece == k, piece, packed)
            xp = jnp.dot(packed.astype(BF16), expand_ref[...], preferred_element_type=F32)
            dd_x = xp[:, 0:SSD_WIDTH]
            ecs_x = xp[:, SSD_WIDTH:2 * SSD_WIDTH]
            cdec_x = ecs_x[CHUNK - 1:CHUNK, :]
            emit_a()

            for g in range(SSD_GROUPS):
                gc = slice(g * GROUP_WIDTH, (g + 1) * GROUP_WIDTH)
                xs_g = act_ref[:, gc]
                xs_gb = xs_g.astype(BF16)
                b_g = act_ref[:, OFF_B + g * D_STATE:OFF_B + (g + 1) * D_STATE].astype(BF16)
                c_g = act_ref[:, OFF_C + g * D_STATE:OFF_C + (g + 1) * D_STATE].astype(BF16)
                cb_g = lax.dot_general(c_g, b_g, (((1,), (1,)), ((), ())), preferred_element_type=F32)
                g_blocks = []
                x_blocks = []
                for r in range(HEADS_PER_GROUP):
                    hh = g * HEADS_PER_GROUP + r
                    seg = cs[:, hh:hh + 1] - cs_t[hh:hh + 1, :]
                    dec = jnp.exp(jnp.where(causal, seg, -jnp.inf))
                    g_blocks.append((cb_g * dec * dt_t[hh:hh + 1, :]).astype(BF16))
                    x_blocks.append(jnp.where(lane_head_in_group == r, xs_gb, jnp.zeros_like(xs_gb)))
                y_diag = jnp.dot(jnp.concatenate(g_blocks, axis=1), jnp.concatenate(x_blocks, axis=0),
                                 preferred_element_type=F32)
                st = state_ref[g]
                y_off = jnp.dot(c_g, st.astype(BF16), preferred_element_type=F32) * ecs_x[:, gc]
                xdd_g = (xs_g * dd_x[:, gc]).astype(BF16)
                st_new = lax.dot_general(b_g, xdd_g, (((0,), (0,)), ((), ())), preferred_element_type=F32)
                state_ref[g] = st * cdec_x[:, gc] + st_new
                y = y_diag + y_off + xs_g * dsk_ref[:, gc]
                z = proj_b_ref[rows, OFF_Z + g * GROUP_WIDTH:OFF_Z + (g + 1) * GROUP_WIDTH]
                yz = y * jax.nn.silu(z)
                yn = yz * lax.rsqrt(jnp.mean(yz * yz, axis=-1, keepdims=True) + EPS) * nw_ref[:, gc]
                ycat_ref[rows, GMLP_WIDTH + g * GROUP_WIDTH:GMLP_WIDTH + (g + 1) * GROUP_WIDTH] = yn.astype(BF16)
                if c == n_chunks - 1 and g % groups_per_block == groups_per_block - 1:
                    out_task(GMLP_WIDTH // OUT_K_BLOCK + g // groups_per_block)
                emit_a()

        emit_a(len(a_tasks))

        xbc_a_ref[:, 0:PAD_ROWS, :] = xbc_b_ref[:, row_tile:row_tile + PAD_ROWS, :]

        mx = mixed_ref[...]
        mn = mx * lax.rsqrt(jnp.mean(mx * mx, axis=-1, keepdims=True) + EPS) * post_w_ref[...]
        out_ref[0] = xkeep_b_ref[...] + mn

    @pl.when(s % 2 == 0)
    def _even_step():
        run(proj0_ref, xbc0_ref, dtraw0_ref, xkeep0_ref, proj1_ref, xbc1_ref, dtraw1_ref, xkeep1_ref)

    @pl.when(s % 2 == 1)
    def _odd_step():
        run(proj1_ref, xbc1_ref, dtraw1_ref, xkeep1_ref, proj0_ref, xbc0_ref, dtraw0_ref, xkeep0_ref)


def _vmem_limit_bytes(row_tile):
    weights = (D_MODEL * MAIN_COLS + D_MODEL * LANES + MIX_WIDTH * D_MODEL) * 2
    small = (GMLP_HEADS * CHUNK * CHUNK + CHUNK * GMLP_WIDTH) * 4 + 64 * 1024
    io = 2 * 2 * row_tile * D_MODEL * 4
    per_tile = (row_tile * OFF_XBC + (PAD_ROWS + row_tile) * CONV_DIM + row_tile * LANES + row_tile * D_MODEL) * 4
    scratch = (2 * per_tile + CHUNK * CONV_DIM * 4 + row_tile * MIX_WIDTH * 2
               + SSD_GROUPS * D_STATE * GROUP_WIDTH * 4 + GMLP_HEADS * CHUNK * CHUNK * 2
               + LANES * 2 * SSD_WIDTH * 2 + row_tile * D_MODEL * 4)
    temporaries = 8 * 1024 * 1024
    return weights + small + io + scratch + temporaries


def _layer(x, pre_w, w_main, w_dt, vw, vb, ws, bsf, gw, cw, cb, dtb, alog, dsk, nw, w_out, post_w,
           *, row_tile=ROW_TILE):
    bsz, seq, _ = x.shape
    assert seq % row_tile == 0 and row_tile % CHUNK == 0
    tiles_per_seq = seq // row_tile
    n_tiles = bsz * tiles_per_seq

    def const(shape):
        nd = len(shape)
        return pl.BlockSpec(shape, lambda s: (0,) * nd, pipeline_mode=pl.Buffered(1))

    def tile_index(i):
        return (i // tiles_per_seq, i % tiles_per_seq, 0)

    x_spec = pl.BlockSpec((1, row_tile, D_MODEL), lambda s: tile_index(jnp.minimum(s, n_tiles - 1)))
    out_spec = pl.BlockSpec((1, row_tile, D_MODEL), lambda s: tile_index(jnp.maximum(s - 1, 0)))
    in_specs = [
        x_spec,
        const((1, D_MODEL)),
        const((D_MODEL, MAIN_COLS)),
        const((D_MODEL, LANES)),
        const((1, GMLP_WIDTH)),
        const((1, GMLP_WIDTH)),
        const((GMLP_HEADS, CHUNK, CHUNK)),
        const((CHUNK, GMLP_WIDTH)),
        const((1, GMLP_WIDTH)),
        const((CONV_WIDTH, CONV_DIM)),
        const((1, CONV_DIM)),
        const((1, LANES)),
        const((1, LANES)),
        const((1, SSD_WIDTH)),
        const((1, SSD_WIDTH)),
        const((MIX_WIDTH, D_MODEL)),
        const((1, D_MODEL)),
    ]
    proj_buf = pltpu.VMEM((row_tile, OFF_XBC), F32)
    xbc_buf = pltpu.VMEM((CONV_DIM // LANES, PAD_ROWS + row_tile, LANES), F32)
    dtraw_buf = pltpu.VMEM((row_tile, LANES), F32)
    xkeep_buf = pltpu.VMEM((row_tile, D_MODEL), F32)
    scratch = [
        proj_buf, proj_buf, xbc_buf, xbc_buf, dtraw_buf, dtraw_buf, xkeep_buf, xkeep_buf,
        pltpu.VMEM((CHUNK, CONV_DIM), F32),
        pltpu.VMEM((row_tile, MIX_WIDTH), BF16),
        pltpu.VMEM((SSD_GROUPS, D_STATE, GROUP_WIDTH), F32),
        pltpu.VMEM((GMLP_HEADS, CHUNK, CHUNK), BF16),
        pltpu.VMEM((LANES, 2 * SSD_WIDTH), BF16),
        pltpu.VMEM((row_tile, D_MODEL), F32),
    ]
    return pl.pallas_call(
        functools.partial(_layer_kernel, row_tile=row_tile, tiles_per_seq=tiles_per_seq),
        grid=(n_tiles + 1,),
        in_specs=in_specs,
        out_specs=out_spec,
        out_shape=jax.ShapeDtypeStruct(x.shape, x.dtype),
        scratch_shapes=scratch,
        compiler_params=pltpu.CompilerParams(
            dimension_semantics=("arbitrary",),
            vmem_limit_bytes=_vmem_limit_bytes(row_tile)),
        name="hybrid_layer",
    )(x, pre_w, w_main, w_dt, vw, vb, ws, bsf, gw, cw, cb, dtb, alog, dsk, nw, w_out, post_w)


def kernel(x, pre_norm_w, w_in, gmlp_v_norm_w, gmlp_v_norm_b, gmlp_ws, gmlp_bs, gmlp_norm_w,
           conv_w, conv_b, dt_bias, a_log, d_skip, ssd_norm_w, w_out, post_norm_w):
    depth = w_in.shape[0]
    h = x
    for i in range(depth):
        w_main = w_in[i, :, :MAIN_COLS].astype(BF16)
        w_dt = jnp.tile(w_in[i, :, MAIN_COLS:], (1, HEAD_REPLICAS)).astype(BF16)
        bsf = jnp.repeat(gmlp_bs[i].T, GMLP_HEAD_DIM, axis=1)
        h = _layer(
            h,
            pre_norm_w[i][None, :],
            w_main,
            w_dt,
            gmlp_v_norm_w[i][None, :],
            gmlp_v_norm_b[i][None, :],
            gmlp_ws[i],
            bsf,
            gmlp_norm_w[i][None, :],
            conv_w[i],
            conv_b[i][None, :],
            jnp.tile(dt_bias[i], HEAD_REPLICAS)[None, :],
            jnp.tile(a_log[i], HEAD_REPLICAS)[None, :],
            jnp.repeat(d_skip[i], SSD_HEAD_DIM)[None, :],
            ssd_norm_w[i][None, :],
            w_out[i].astype(BF16),
            post_norm_w[i][None, :],
        )
    return h
```

```python
import functools

import jax
import jax.numpy as jnp
from jax import lax
from jax.experimental import pallas as pl
from jax.experimental.pallas import tpu as pltpu

D_MODEL = 1024
CHUNK = 128
GMLP_WIDTH = 1024
GMLP_HEADS = 8
GMLP_HEAD_DIM = 128
SSD_WIDTH = 1024
SSD_HEAD_DIM = 64
SSD_HEADS = 16
SSD_GROUPS = 4
HEADS_PER_GROUP = 4
GROUP_WIDTH = SSD_WIDTH // SSD_GROUPS
D_STATE = 128
CONV_WIDTH = 4
CONV_DIM = SSD_WIDTH + 2 * SSD_GROUPS * D_STATE
MIX_WIDTH = GMLP_WIDTH + SSD_WIDTH
EPS = 1e-6

OFF_U = 0
OFF_V = GMLP_WIDTH
OFF_ZG = 2 * GMLP_WIDTH
OFF_Z = 3 * GMLP_WIDTH
OFF_XBC = 3 * GMLP_WIDTH + SSD_WIDTH
MAIN_COLS = OFF_XBC + CONV_DIM
OFF_B = SSD_WIDTH
OFF_C = SSD_WIDTH + SSD_GROUPS * D_STATE

LANES = 128
SUBLANES = 8
MXU_DIM = 256
HEAD_REPLICAS = LANES // SSD_HEADS
PAD_ROWS = SUBLANES
ROW_TILE = 256
PROJ_COL_BLOCK = MXU_DIM
OUT_K_BLOCK = 2 * MXU_DIM
N_SPLIT = 3

F32 = jnp.float32
BF16 = jnp.bfloat16


def _split3(x):
    hi = x.astype(BF16).astype(F32)
    r1 = x - hi
    mid = r1.astype(BF16).astype(F32)
    lo = r1 - mid
    return hi, mid, lo


def _layer_kernel(x_ref, pre_w_ref, w_main_ref, w_dt_ref, vw_ref, vb_ref, ws_ref, bsf_ref,
                  gw_ref, cw_ref, cb_ref, dtb_ref, alog_ref, dsk_ref, nw_ref, w_out_ref,
                  post_w_ref, out_ref,
                  proj0_ref, proj1_ref, xbc0_ref, xbc1_ref, dtraw0_ref, dtraw1_ref, xkeep0_ref, xkeep1_ref,
                  act_ref, ycat_ref, state_ref, wsm_ref, expand_ref, mixed_ref,
                  *, row_tile, tiles_per_seq):
    n_chunks = row_tile // CHUNK
    s = pl.program_id(0)
    t_prev = (s + tiles_per_seq - 1) % tiles_per_seq

    row_i = lax.broadcasted_iota(jnp.int32, (CHUNK, CHUNK), 0)
    col_i = lax.broadcasted_iota(jnp.int32, (CHUNK, CHUNK), 1)
    causal = row_i >= col_i

    @pl.when(s == 0)
    def _first_step():
        proj1_ref[...] = jnp.zeros_like(proj1_ref)
        xbc1_ref[...] = jnp.zeros_like(xbc1_ref)
        dtraw1_ref[...] = jnp.zeros_like(dtraw1_ref)
        xkeep1_ref[...] = jnp.zeros_like(xkeep1_ref)
        state_ref[...] = jnp.zeros_like(state_ref)
        for hd in range(GMLP_HEADS):
            wsm_ref[hd] = jnp.where(causal, ws_ref[hd], 0.0).astype(BF16)
        ei = lax.broadcasted_iota(jnp.int32, (LANES, 2 * SSD_WIDTH), 0)
        ej = lax.broadcasted_iota(jnp.int32, (LANES, 2 * SSD_WIDTH), 1)
        which = ej // SSD_WIDTH
        head = (ej % SSD_WIDTH) // SSD_HEAD_DIM
        grp = ei // SSD_HEADS
        hit = (ei % SSD_HEADS == head) & (grp >= which * N_SPLIT) & (grp < (which + 1) * N_SPLIT)
        expand_ref[...] = jnp.where(hit, 1.0, 0.0).astype(BF16)

    def run(proj_a_ref, xbc_a_ref, dtraw_a_ref, xkeep_a_ref, proj_b_ref, xbc_b_ref, dtraw_b_ref, xkeep_b_ref):
        @pl.when(t_prev == 0)
        def _start_of_sequence():
            state_ref[...] = jnp.zeros_like(state_ref)
            xbc_b_ref[:, 0:PAD_ROWS, :] = jnp.zeros((CONV_DIM // LANES, PAD_ROWS, LANES), F32)

        x = x_ref[0]
        xkeep_a_ref[...] = x
        h = x * lax.rsqrt(jnp.mean(x * x, axis=-1, keepdims=True) + EPS) * pre_w_ref[...]
        hb = h.astype(BF16)

        def proj_task(j):
            cols = slice(j * PROJ_COL_BLOCK, (j + 1) * PROJ_COL_BLOCK)
            d = jnp.dot(hb, w_main_ref[:, cols], preferred_element_type=F32)
            if cols.start < OFF_XBC:
                proj_a_ref[:, cols] = d
            else:
                for i in range(PROJ_COL_BLOCK // LANES):
                    blk = (cols.start - OFF_XBC) // LANES + i
                    xbc_a_ref[blk, PAD_ROWS:PAD_ROWS + row_tile, :] = d[:, i * LANES:(i + 1) * LANES]

        def dt_task():
            dtraw_a_ref[...] = jnp.dot(hb, w_dt_ref[...], preferred_element_type=F32)

        a_tasks = [dt_task] + [functools.partial(proj_task, j) for j in range(MAIN_COLS // PROJ_COL_BLOCK)]

        def emit_a(n=1):
            for _ in range(n):
                if a_tasks:
                    a_tasks.pop(0)()

        def out_task(kb):
            ks = slice(kb * OUT_K_BLOCK, (kb + 1) * OUT_K_BLOCK)
            d = jnp.dot(ycat_ref[:, ks], w_out_ref[ks, :], preferred_element_type=F32)
            if kb == 0:
                mixed_ref[...] = d
            else:
                mixed_ref[...] += d

        tril_bf = jnp.where(causal, 1.0, 0.0).astype(BF16)
        lane_head_in_group = lax.broadcasted_iota(jnp.int32, (CHUNK, GROUP_WIDTH), 1) // SSD_HEAD_DIM
        lane_piece = col_i // SSD_HEADS
        a_row = -jnp.exp(alog_ref[...])

        emit_a(2)
        vnb = []
        for c in range(n_chunks):
            rows = slice(c * CHUNK, (c + 1) * CHUNK)
            v = proj_b_ref[rows, OFF_V:OFF_V + GMLP_WIDTH]
            mu = jnp.mean(v, axis=-1, keepdims=True)
            vc = v - mu
            vn = vc * lax.rsqrt(jnp.mean(vc * vc, axis=-1, keepdims=True) + EPS)
            vnb.append((vn * vw_ref[...] + vb_ref[...]).astype(BF16))
            emit_a()
        heads_per_block = OUT_K_BLOCK // GMLP_HEAD_DIM
        for kb in range(GMLP_WIDTH // OUT_K_BLOCK):
            for c in range(n_chunks):
                rows = slice(c * CHUNK, (c + 1) * CHUNK)
                for hd in range(kb * heads_per_block, (kb + 1) * heads_per_block):
                    cs_ = slice(hd * GMLP_HEAD_DIM, (hd + 1) * GMLP_HEAD_DIM)
                    sg = jnp.dot(wsm_ref[hd], vnb[c][:, cs_], preferred_element_type=F32) + bsf_ref[:, cs_]
                    u = proj_b_ref[rows, OFF_U + hd * GMLP_HEAD_DIM:OFF_U + (hd + 1) * GMLP_HEAD_DIM]
                    zg = proj_b_ref[rows, OFF_ZG + hd * GMLP_HEAD_DIM:OFF_ZG + (hd + 1) * GMLP_HEAD_DIM]
                    y = u * sg * jax.nn.silu(zg)
                    yn = y * lax.rsqrt(jnp.mean(y * y, axis=-1, keepdims=True) + EPS) * gw_ref[:, cs_]
                    ycat_ref[rows, cs_] = yn.astype(BF16)
            out_task(kb)
            emit_a()

        groups_per_block = OUT_K_BLOCK // GROUP_WIDTH
        for c in range(n_chunks):
            rows = slice(c * CHUNK, (c + 1) * CHUNK)
            r0 = PAD_ROWS + c * CHUNK

            for j in range(CONV_DIM // LANES):
                cc = slice(j * LANES, (j + 1) * LANES)
                acc = cb_ref[:, cc] + cw_ref[3:4, cc] * xbc_b_ref[j, r0:r0 + CHUNK, :]
                acc = acc + cw_ref[2:3, cc] * xbc_b_ref[j, r0 - 1:r0 - 1 + CHUNK, :]
                acc = acc + cw_ref[1:2, cc] * xbc_b_ref[j, r0 - 2:r0 - 2 + CHUNK, :]
                acc = acc + cw_ref[0:1, cc] * xbc_b_ref[j, r0 - 3:r0 - 3 + CHUNK, :]
                act_ref[:, cc] = jax.nn.silu(acc)
                if j % 4 == 3:
                    emit_a()

            dt = jax.nn.softplus(dtraw_b_ref[rows, :] + dtb_ref[...])
            da = dt * a_row
            da_pieces = jnp.concatenate(_split3(da), axis=1).astype(BF16)
            cs3 = jnp.dot(tril_bf, da_pieces, preferred_element_type=F32)
            cs = cs3[:, 0:LANES] + cs3[:, LANES:2 * LANES] + cs3[:, 2 * LANES:3 * LANES]
            cs_t = cs.T
            dt_t = dt.T
            cs_last = cs[CHUNK - 1:CHUNK, :]
            ecs = jnp.exp(cs)
            dd = dt * jnp.exp(cs_last - cs)
            pieces = _split3(dd) + _split3(ecs)
            packed = jnp.zeros((CHUNK, LANES), F32)
            for k, piece in enumerate(pieces):
                packed = jnp.where(lane_piece == k, piece, packed)
            xp = jnp.dot(packed.astype(BF16), expand_ref[...], preferred_element_type=F32)
            dd_x = xp[:, 0:SSD_WIDTH]
            ecs_x = xp[:, SSD_WIDTH:2 * SSD_WIDTH]
            cdec_x = ecs_x[CHUNK - 1:CHUNK, :]
            emit_a()

            for g in range(SSD_GROUPS):
                gc = slice(g * GROUP_WIDTH, (g + 1) * GROUP_WIDTH)
                xs_g = act_ref[:, gc]
                xs_gb = xs_g.astype(BF16)
                b_g = act_ref[:, OFF_B + g * D_STATE:OFF_B + (g + 1) * D_STATE].astype(BF16)
                c_g = act_ref[:, OFF_C + g * D_STATE:OFF_C + (g + 1) * D_STATE].astype(BF16)
                cb_g = lax.dot_general(c_g, b_g, (((1,), (1,)), ((), ())), preferred_element_type=F32)
                g_blocks = []
                x_blocks = []
                for r in range(HEADS_PER_GROUP):
                    hh = g * HEADS_PER_GROUP + r
                    seg = cs[:, hh:hh + 1] - cs_t[hh:hh + 1, :]
                    dec = jnp.exp(jnp.where(causal, seg, -jnp.inf))
                    g_blocks.append((cb_g * dec * dt_t[hh:hh + 1, :]).astype(BF16))
                    x_blocks.append(jnp.where(lane_head_in_group == r, xs_gb, jnp.zeros_like(xs_gb)))
                y_diag = jnp.dot(jnp.concatenate(g_blocks, axis=1), jnp.concatenate(x_blocks, axis=0),
                                 preferred_element_type=F32)
                st = state_ref[g]
                y_off = jnp.dot(c_g, st.astype(BF16), preferred_element_type=F32) * ecs_x[:, gc]
                xdd_g = (xs_g * dd_x[:, gc]).astype(BF16)
                st_new = lax.dot_general(b_g, xdd_g, (((0,), (0,)), ((), ())), preferred_element_type=F32)
                state_ref[g] = st * cdec_x[:, gc] + st_new
                y = y_diag + y_off + xs_g * dsk_ref[:, gc]
                z = proj_b_ref[rows, OFF_Z + g * GROUP_WIDTH:OFF_Z + (g + 1) * GROUP_WIDTH]
                yz = y * jax.nn.silu(z)
                yn = yz * lax.rsqrt(jnp.mean(yz * yz, axis=-1, keepdims=True) + EPS) * nw_ref[:, gc]
                ycat_ref[rows, GMLP_WIDTH + g * GROUP_WIDTH:GMLP_WIDTH + (g + 1) * GROUP_WIDTH] = yn.astype(BF16)
                if c == n_chunks - 1 and g % groups_per_block == groups_per_block - 1:
                    out_task(GMLP_WIDTH // OUT_K_BLOCK + g // groups_per_block)
                emit_a()

        emit_a(len(a_tasks))

        xbc_a_ref[:, 0:PAD_ROWS, :] = xbc_b_ref[:, row_tile:row_tile + PAD_ROWS, :]

        mx = mixed_ref[...]
        mn = mx * lax.rsqrt(jnp.mean(mx * mx, axis=-1, keepdims=True) + EPS) * post_w_ref[...]
        out_ref[0] = xkeep_b_ref[...] + mn

    @pl.when(s % 2 == 0)
    def _even_step():
        run(proj0_ref, xbc0_ref, dtraw0_ref, xkeep0_ref, proj1_ref, xbc1_ref, dtraw1_ref, xkeep1_ref)

    @pl.when(s % 2 == 1)
    def _odd_step():
        run(proj1_ref, xbc1_ref, dtraw1_ref, xkeep1_ref, proj0_ref, xbc0_ref, dtraw0_ref, xkeep0_ref)


def _vmem_limit_bytes(row_tile):
    weights = (D_MODEL * MAIN_COLS + D_MODEL * LANES + MIX_WIDTH * D_MODEL) * 2
    small = (GMLP_HEADS * CHUNK * CHUNK + CHUNK * GMLP_WIDTH) * 4 + 64 * 1024
    io = 2 * 2 * row_tile * D_MODEL * 4
    per_tile = (row_tile * OFF_XBC + (PAD_ROWS + row_tile) * CONV_DIM + row_tile * LANES + row_tile * D_MODEL) * 4
    scratch = (2 * per_tile + CHUNK * CONV_DIM * 4 + row_tile * MIX_WIDTH * 2
               + SSD_GROUPS * D_STATE * GROUP_WIDTH * 4 + GMLP_HEADS * CHUNK * CHUNK * 2
               + LANES * 2 * SSD_WIDTH * 2 + row_tile * D_MODEL * 4)
    temporaries = 8 * 1024 * 1024
    return weights + small + io + scratch + temporaries


def _layer(layer, x, pre_w, w_main, w_dt, vw, vb, ws, bsf, gw, cw, cb, dtb, alog, dsk, nw, w_out, post_w,
           *, row_tile=ROW_TILE):
    bsz, seq, _ = x.shape
    assert seq % row_tile == 0 and row_tile % CHUNK == 0
    tiles_per_seq = seq // row_tile
    n_tiles = bsz * tiles_per_seq

    def const(shape):
        nd = len(shape)
        return pl.BlockSpec(shape, lambda s: (0,) * nd, pipeline_mode=pl.Buffered(1))

    def of_layer(shape):
        return pl.BlockSpec((None,) + shape, lambda s: (layer, 0, 0), pipeline_mode=pl.Buffered(1))

    def tile_index(i):
        return (i // tiles_per_seq, i % tiles_per_seq, 0)

    x_spec = pl.BlockSpec((1, row_tile, D_MODEL), lambda s: tile_index(jnp.minimum(s, n_tiles - 1)))
    out_spec = pl.BlockSpec((1, row_tile, D_MODEL), lambda s: tile_index(jnp.maximum(s - 1, 0)))
    in_specs = [
        x_spec,
        const((1, D_MODEL)),
        of_layer((D_MODEL, MAIN_COLS)),
        of_layer((D_MODEL, LANES)),
        const((1, GMLP_WIDTH)),
        const((1, GMLP_WIDTH)),
        const((GMLP_HEADS, CHUNK, CHUNK)),
        const((CHUNK, GMLP_WIDTH)),
        const((1, GMLP_WIDTH)),
        const((CONV_WIDTH, CONV_DIM)),
        const((1, CONV_DIM)),
        const((1, LANES)),
        const((1, LANES)),
        const((1, SSD_WIDTH)),
        const((1, SSD_WIDTH)),
        of_layer((MIX_WIDTH, D_MODEL)),
        const((1, D_MODEL)),
    ]
    proj_buf = pltpu.VMEM((row_tile, OFF_XBC), F32)
    xbc_buf = pltpu.VMEM((CONV_DIM // LANES, PAD_ROWS + row_tile, LANES), F32)
    dtraw_buf = pltpu.VMEM((row_tile, LANES), F32)
    xkeep_buf = pltpu.VMEM((row_tile, D_MODEL), F32)
    scratch = [
        proj_buf, proj_buf, xbc_buf, xbc_buf, dtraw_buf, dtraw_buf, xkeep_buf, xkeep_buf,
        pltpu.VMEM((CHUNK, CONV_DIM), F32),
        pltpu.VMEM((row_tile, MIX_WIDTH), BF16),
        pltpu.VMEM((SSD_GROUPS, D_STATE, GROUP_WIDTH), F32),
        pltpu.VMEM((GMLP_HEADS, CHUNK, CHUNK), BF16),
        pltpu.VMEM((LANES, 2 * SSD_WIDTH), BF16),
        pltpu.VMEM((row_tile, D_MODEL), F32),
    ]
    return pl.pallas_call(
        functools.partial(_layer_kernel, row_tile=row_tile, tiles_per_seq=tiles_per_seq),
        grid=(n_tiles + 1,),
        in_specs=in_specs,
        out_specs=out_spec,
        out_shape=jax.ShapeDtypeStruct(x.shape, x.dtype),
        scratch_shapes=scratch,
        compiler_params=pltpu.CompilerParams(
            dimension_semantics=("arbitrary",),
            vmem_limit_bytes=_vmem_limit_bytes(row_tile)),
        name="hybrid_layer",
    )(x, pre_w, w_main, w_dt, vw, vb, ws, bsf, gw, cw, cb, dtb, alog, dsk, nw, w_out, post_w)


def kernel(x, pre_norm_w, w_in, gmlp_v_norm_w, gmlp_v_norm_b, gmlp_ws, gmlp_bs, gmlp_norm_w,
           conv_w, conv_b, dt_bias, a_log, d_skip, ssd_norm_w, w_out, post_norm_w):
    depth = w_in.shape[0]
    w_main = w_in[:, :, :MAIN_COLS].astype(BF16)
    w_dt = jnp.tile(w_in[:, :, MAIN_COLS:], (1, 1, HEAD_REPLICAS)).astype(BF16)
    w_out_bf = w_out.astype(BF16)
    h = x
    for i in range(depth):
        bsf = jnp.repeat(gmlp_bs[i].T, GMLP_HEAD_DIM, axis=1)
        h = _layer(
            i,
            h,
            pre_norm_w[i][None, :],
            w_main,
            w_dt,
            gmlp_v_norm_w[i][None, :],
            gmlp_v_norm_b[i][None, :],
            gmlp_ws[i],
            bsf,
            gmlp_norm_w[i][None, :],
            conv_w[i],
            conv_b[i][None, :],
            jnp.tile(dt_bias[i], HEAD_REPLICAS)[None, :],
            jnp.tile(a_log[i], HEAD_REPLICAS)[None, :],
            jnp.repeat(d_skip[i], SSD_HEAD_DIM)[None, :],
            ssd_norm_w[i][None, :],
            w_out_bf,
            post_norm_w[i][None, :],
        )
    return h
```

```python
import functools

import jax
import jax.numpy as jnp
from jax import lax
from jax.experimental import pallas as pl
from jax.experimental.pallas import tpu as pltpu

D_MODEL = 1024
CHUNK = 128
GMLP_WIDTH = 1024
GMLP_HEADS = 8
GMLP_HEAD_DIM = 128
SSD_WIDTH = 1024
SSD_HEAD_DIM = 64
SSD_HEADS = 16
SSD_GROUPS = 4
HEADS_PER_GROUP = 4
GROUP_WIDTH = SSD_WIDTH // SSD_GROUPS
D_STATE = 128
CONV_WIDTH = 4
CONV_DIM = SSD_WIDTH + 2 * SSD_GROUPS * D_STATE
MIX_WIDTH = GMLP_WIDTH + SSD_WIDTH
EPS = 1e-6

OFF_U = 0
OFF_V = GMLP_WIDTH
OFF_ZG = 2 * GMLP_WIDTH
OFF_Z = 3 * GMLP_WIDTH
OFF_XBC = 3 * GMLP_WIDTH + SSD_WIDTH
MAIN_COLS = OFF_XBC + CONV_DIM
OFF_B = SSD_WIDTH
OFF_C = SSD_WIDTH + SSD_GROUPS * D_STATE

LANES = 128
SUBLANES = 8
MXU_DIM = 256
HEAD_REPLICAS = LANES // SSD_HEADS
PAD_ROWS = SUBLANES
ROW_TILE = 256
PROJ_COL_BLOCK = MXU_DIM
OUT_K_BLOCK = 2 * MXU_DIM
N_SPLIT = 3

F32 = jnp.float32
BF16 = jnp.bfloat16


def _split3(x):
    hi = x.astype(BF16).astype(F32)
    r1 = x - hi
    mid = r1.astype(BF16).astype(F32)
    lo = r1 - mid
    return hi, mid, lo


def _layer_kernel(x_ref, pre_w_ref, w_main_ref, w_dt_ref, vw_ref, vb_ref, ws_ref, bsf_ref,
                  gw_ref, cw_ref, cb_ref, dtb_ref, alog_ref, dsk_ref, nw_ref, w_out_ref,
                  post_w_ref, out_ref,
                  proj0_ref, proj1_ref, xbc0_ref, xbc1_ref, dtraw0_ref, dtraw1_ref, xkeep0_ref, xkeep1_ref,
                  act_ref, ycat_ref, state_ref, wsm_ref, expand_ref, mixed_ref,
                  *, row_tile, tiles_per_seq):
    n_chunks = row_tile // CHUNK
    s = pl.program_id(0)
    t_prev = (s + tiles_per_seq - 1) % tiles_per_seq

    row_i = lax.broadcasted_iota(jnp.int32, (CHUNK, CHUNK), 0)
    col_i = lax.broadcasted_iota(jnp.int32, (CHUNK, CHUNK), 1)
    causal = row_i >= col_i

    @pl.when(s == 0)
    def _first_step():
        proj1_ref[...] = jnp.zeros_like(proj1_ref)
        xbc1_ref[...] = jnp.zeros_like(xbc1_ref)
        dtraw1_ref[...] = jnp.zeros_like(dtraw1_ref)
        xkeep1_ref[...] = jnp.zeros_like(xkeep1_ref)
        state_ref[...] = jnp.zeros_like(state_ref)
        for hd in range(GMLP_HEADS):
            wsm_ref[hd] = jnp.where(causal, ws_ref[hd], 0.0).astype(BF16)
        ei = lax.broadcasted_iota(jnp.int32, (LANES, 2 * SSD_WIDTH), 0)
        ej = lax.broadcasted_iota(jnp.int32, (LANES, 2 * SSD_WIDTH), 1)
        which = ej // SSD_WIDTH
        head = (ej % SSD_WIDTH) // SSD_HEAD_DIM
        grp = ei // SSD_HEADS
        hit = (ei % SSD_HEADS == head) & (grp >= which * N_SPLIT) & (grp < (which + 1) * N_SPLIT)
        expand_ref[...] = jnp.where(hit, 1.0, 0.0).astype(BF16)

    def run(proj_a_ref, xbc_a_ref, dtraw_a_ref, xkeep_a_ref, proj_b_ref, xbc_b_ref, dtraw_b_ref, xkeep_b_ref):
        @pl.when(t_prev == 0)
        def _start_of_sequence():
            state_ref[...] = jnp.zeros_like(state_ref)
            xbc_b_ref[:, 0:PAD_ROWS, :] = jnp.zeros((CONV_DIM // LANES, PAD_ROWS, LANES), F32)

        x = x_ref[0]
        xkeep_a_ref[...] = x
        h = x * lax.rsqrt(jnp.mean(x * x, axis=-1, keepdims=True) + EPS) * pre_w_ref[...]
        hb = h.astype(BF16)

        def proj_task(j):
            cols = slice(j * PROJ_COL_BLOCK, (j + 1) * PROJ_COL_BLOCK)
            d = jnp.dot(hb, w_main_ref[:, cols], preferred_element_type=F32)
            if cols.start < OFF_XBC:
                proj_a_ref[:, cols] = d
            else:
                for i in range(PROJ_COL_BLOCK // LANES):
                    blk = (cols.start - OFF_XBC) // LANES + i
                    xbc_a_ref[blk, PAD_ROWS:PAD_ROWS + row_tile, :] = d[:, i * LANES:(i + 1) * LANES]

        def dt_task():
            dtraw_a_ref[...] = jnp.dot(hb, w_dt_ref[...], preferred_element_type=F32)

        a_tasks = [dt_task] + [functools.partial(proj_task, j) for j in range(MAIN_COLS // PROJ_COL_BLOCK)]

        def emit_a(n=1):
            for _ in range(n):
                if a_tasks:
                    a_tasks.pop(0)()

        def out_task(kb):
            ks = slice(kb * OUT_K_BLOCK, (kb + 1) * OUT_K_BLOCK)
            d = jnp.dot(ycat_ref[:, ks], w_out_ref[ks, :], preferred_element_type=F32)
            if kb == 0:
                mixed_ref[...] = d
            else:
                mixed_ref[...] += d

        tril_bf = jnp.where(causal, 1.0, 0.0).astype(BF16)
        lane_head_in_group = lax.broadcasted_iota(jnp.int32, (CHUNK, GROUP_WIDTH), 1) // SSD_HEAD_DIM
        lane_piece = col_i // SSD_HEADS
        a_row = -jnp.exp(alog_ref[...])

        emit_a(2)
        vnb = []
        for c in range(n_chunks):
            rows = slice(c * CHUNK, (c + 1) * CHUNK)
            v = proj_b_ref[rows, OFF_V:OFF_V + GMLP_WIDTH]
            mu = jnp.mean(v, axis=-1, keepdims=True)
            vc = v - mu
            vn = vc * lax.rsqrt(jnp.mean(vc * vc, axis=-1, keepdims=True) + EPS)
            vnb.append((vn * vw_ref[...] + vb_ref[...]).astype(BF16))
            emit_a()
        heads_per_block = OUT_K_BLOCK // GMLP_HEAD_DIM
        for kb in range(GMLP_WIDTH // OUT_K_BLOCK):
            for c in range(n_chunks):
                rows = slice(c * CHUNK, (c + 1) * CHUNK)
                for hd in range(kb * heads_per_block, (kb + 1) * heads_per_block):
                    cs_ = slice(hd * GMLP_HEAD_DIM, (hd + 1) * GMLP_HEAD_DIM)
                    sg = jnp.dot(wsm_ref[hd], vnb[c][:, cs_], preferred_element_type=F32) + bsf_ref[:, cs_]
                    u = proj_b_ref[rows, OFF_U + hd * GMLP_HEAD_DIM:OFF_U + (hd + 1) * GMLP_HEAD_DIM]
                    zg = proj_b_ref[rows, OFF_ZG + hd * GMLP_HEAD_DIM:OFF_ZG + (hd + 1) * GMLP_HEAD_DIM]
                    y = u * sg * jax.nn.silu(zg)
                    yn = y * lax.rsqrt(jnp.mean(y * y, axis=-1, keepdims=True) + EPS) * gw_ref[:, cs_]
                    ycat_ref[rows, cs_] = yn.astype(BF16)
            out_task(kb)
            emit_a()

        groups_per_block = OUT_K_BLOCK // GROUP_WIDTH
        for c in range(n_chunks):
            rows = slice(c * CHUNK, (c + 1) * CHUNK)
            r0 = PAD_ROWS + c * CHUNK

            for j in range(CONV_DIM // LANES):
                cc = slice(j * LANES, (j + 1) * LANES)
                acc = cb_ref[:, cc] + cw_ref[3:4, cc] * xbc_b_ref[j, r0:r0 + CHUNK, :]
                acc = acc + cw_ref[2:3, cc] * xbc_b_ref[j, r0 - 1:r0 - 1 + CHUNK, :]
                acc = acc + cw_ref[1:2, cc] * xbc_b_ref[j, r0 - 2:r0 - 2 + CHUNK, :]
                acc = acc + cw_ref[0:1, cc] * xbc_b_ref[j, r0 - 3:r0 - 3 + CHUNK, :]
                act_ref[:, cc] = jax.nn.silu(acc)
                if j % 4 == 3:
                    emit_a()

            dt = jax.nn.softplus(dtraw_b_ref[rows, :] + dtb_ref[...])
            da = dt * a_row
            da_pieces = jnp.concatenate(_split3(da), axis=1).astype(BF16)
            cs3 = jnp.dot(tril_bf, da_pieces, preferred_element_type=F32)
            cs = cs3[:, 0:LANES] + cs3[:, LANES:2 * LANES] + cs3[:, 2 * LANES:3 * LANES]
            cs_t = cs.T
            dt_t = dt.T
            cs_last = cs[CHUNK - 1:CHUNK, :]
            ecs = jnp.exp(cs)
            dd = dt * jnp.exp(cs_last - cs)
            pieces = _split3(dd) + _split3(ecs)
            packed = jnp.zeros((CHUNK, LANES), F32)
            for k, piece in enumerate(pieces):
                packed = jnp.where(lane_piece == k, piece, packed)
            xp = jnp.dot(packed.astype(BF16), expand_ref[...], preferred_element_type=F32)
            dd_x = xp[:, 0:SSD_WIDTH]
            ecs_x = xp[:, SSD_WIDTH:2 * SSD_WIDTH]
            cdec_x = ecs_x[CHUNK - 1:CHUNK, :]
            emit_a()

            for g in range(SSD_GROUPS):
                gc = slice(g * GROUP_WIDTH, (g + 1) * GROUP_WIDTH)
                xs_g = act_ref[:, gc]
                xs_gb = xs_g.astype(BF16)
                b_g = act_ref[:, OFF_B + g * D_STATE:OFF_B + (g + 1) * D_STATE].astype(BF16)
                c_g = act_ref[:, OFF_C + g * D_STATE:OFF_C + (g + 1) * D_STATE].astype(BF16)
                cb_g = lax.dot_general(c_g, b_g, (((1,), (1,)), ((), ())), preferred_element_type=F32)
                g_blocks = []
                x_blocks = []
                for r in range(HEADS_PER_GROUP):
                    hh = g * HEADS_PER_GROUP + r
                    seg = cs[:, hh:hh + 1] - cs_t[hh:hh + 1, :]
                    dec = jnp.exp(jnp.where(causal, seg, -jnp.inf))
                    g_blocks.append((cb_g * dec * dt_t[hh:hh + 1, :]).astype(BF16))
                    x_blocks.append(jnp.where(lane_head_in_group == r, xs_gb, jnp.zeros_like(xs_gb)))
                y_diag = jnp.dot(jnp.concatenate(g_blocks, axis=1), jnp.concatenate(x_blocks, axis=0),
                                 preferred_element_type=F32)
                st = state_ref[g]
                y_off = jnp.dot(c_g, st.astype(BF16), preferred_element_type=F32) * ecs_x[:, gc]
                xdd_g = (xs_g * dd_x[:, gc]).astype(BF16)
                st_new = lax.dot_general(b_g, xdd_g, (((0,), (0,)), ((), ())), preferred_element_type=F32)
                state_ref[g] = st * cdec_x[:, gc] + st_new
                y = y_diag + y_off + xs_g * dsk_ref[:, gc]
                z = proj_b_ref[rows, OFF_Z + g * GROUP_WIDTH:OFF_Z + (g + 1) * GROUP_WIDTH]
                yz = y * jax.nn.silu(z)
                yn = yz * lax.rsqrt(jnp.mean(yz * yz, axis=-1, keepdims=True) + EPS) * nw_ref[:, gc]
                ycat_ref[rows, GMLP_WIDTH + g * GROUP_WIDTH:GMLP_WIDTH + (g + 1) * GROUP_WIDTH] = yn.astype(BF16)
                if c == n_chunks - 1 and g % groups_per_block == groups_per_block - 1:
                    out_task(GMLP_WIDTH // OUT_K_BLOCK + g // groups_per_block)
                emit_a()

        emit_a(len(a_tasks))

        xbc_a_ref[:, 0:PAD_ROWS, :] = xbc_b_ref[:, row_tile:row_tile + PAD_ROWS, :]

        mx = mixed_ref[...]
        mn = mx * lax.rsqrt(jnp.mean(mx * mx, axis=-1, keepdims=True) + EPS) * post_w_ref[...]
        out_ref[0] = xkeep_b_ref[...] + mn

    @pl.when(s % 2 == 0)
    def _even_step():
        run(proj0_ref, xbc0_ref, dtraw0_ref, xkeep0_ref, proj1_ref, xbc1_ref, dtraw1_ref, xkeep1_ref)

    @pl.when(s % 2 == 1)
    def _odd_step():
        run(proj1_ref, xbc1_ref, dtraw1_ref, xkeep1_ref, proj0_ref, xbc0_ref, dtraw0_ref, xkeep0_ref)


def _vmem_limit_bytes(row_tile):
    weights = (D_MODEL * MAIN_COLS + D_MODEL * LANES + MIX_WIDTH * D_MODEL) * 2
    small = (GMLP_HEADS * CHUNK * CHUNK + CHUNK * GMLP_WIDTH) * 4 + 64 * 1024
    io = 2 * 2 * row_tile * D_MODEL * 4
    per_tile = (row_tile * OFF_XBC + (PAD_ROWS + row_tile) * CONV_DIM + row_tile * LANES + row_tile * D_MODEL) * 4
    scratch = (2 * per_tile + CHUNK * CONV_DIM * 4 + row_tile * MIX_WIDTH * 2
               + SSD_GROUPS * D_STATE * GROUP_WIDTH * 4 + GMLP_HEADS * CHUNK * CHUNK * 2
               + LANES * 2 * SSD_WIDTH * 2 + row_tile * D_MODEL * 4)
    temporaries = 8 * 1024 * 1024
    return weights + small + io + scratch + temporaries


def _layer(layer, x, pre_w, w_main, w_dt, vw, vb, ws, bsf, gw, cw, cb, dtb, alog, dsk, nw, w_out, post_w,
           *, row_tile=ROW_TILE):
    bsz, seq, _ = x.shape
    assert seq % row_tile == 0 and row_tile % CHUNK == 0
    tiles_per_seq = seq // row_tile
    n_tiles = bsz * tiles_per_seq

    def const(shape):
        nd = len(shape)
        return pl.BlockSpec(shape, lambda s: (0,) * nd, pipeline_mode=pl.Buffered(1))

    def of_layer(shape):
        return pl.BlockSpec((None,) + shape, lambda s: (layer, 0, 0), pipeline_mode=pl.Buffered(1))

    def tile_index(i):
        return (i // tiles_per_seq, i % tiles_per_seq, 0)

    x_spec = pl.BlockSpec((1, row_tile, D_MODEL), lambda s: tile_index(jnp.minimum(s, n_tiles - 1)))
    out_spec = pl.BlockSpec((1, row_tile, D_MODEL), lambda s: tile_index(jnp.maximum(s - 1, 0)))
    in_specs = [
        x_spec,
        const((1, D_MODEL)),
        of_layer((D_MODEL, MAIN_COLS)),
        of_layer((D_MODEL, LANES)),
        const((1, GMLP_WIDTH)),
        const((1, GMLP_WIDTH)),
        const((GMLP_HEADS, CHUNK, CHUNK)),
        const((CHUNK, GMLP_WIDTH)),
        const((1, GMLP_WIDTH)),
        const((CONV_WIDTH, CONV_DIM)),
        const((1, CONV_DIM)),
        const((1, LANES)),
        const((1, LANES)),
        const((1, SSD_WIDTH)),
        const((1, SSD_WIDTH)),
        of_layer((MIX_WIDTH, D_MODEL)),
        const((1, D_MODEL)),
    ]
    proj_buf = pltpu.VMEM((row_tile, OFF_XBC), F32)
    xbc_buf = pltpu.VMEM((CONV_DIM // LANES, PAD_ROWS + row_tile, LANES), F32)
    dtraw_buf = pltpu.VMEM((row_tile, LANES), F32)
    xkeep_buf = pltpu.VMEM((row_tile, D_MODEL), F32)
    scratch = [
        proj_buf, proj_buf, xbc_buf, xbc_buf, dtraw_buf, dtraw_buf, xkeep_buf, xkeep_buf,
        pltpu.VMEM((CHUNK, CONV_DIM), F32),
        pltpu.VMEM((row_tile, MIX_WIDTH), BF16),
        pltpu.VMEM((SSD_GROUPS, D_STATE, GROUP_WIDTH), F32),
        pltpu.VMEM((GMLP_HEADS, CHUNK, CHUNK), BF16),
        pltpu.VMEM((LANES, 2 * SSD_WIDTH), BF16),
        pltpu.VMEM((row_tile, D_MODEL), F32),
    ]
    return pl.pallas_call(
        functools.partial(_layer_kernel, row_tile=row_tile, tiles_per_seq=tiles_per_seq),
        grid=(n_tiles + 1,),
        in_specs=in_specs,
        out_specs=out_spec,
        out_shape=jax.ShapeDtypeStruct(x.shape, x.dtype),
        scratch_shapes=scratch,
        compiler_params=pltpu.CompilerParams(
            dimension_semantics=("arbitrary",),
            vmem_limit_bytes=_vmem_limit_bytes(row_tile)),
        name="hybrid_layer",
    )(x, pre_w, w_main, w_dt, vw, vb, ws, bsf, gw, cw, cb, dtb, alog, dsk, nw, w_out, post_w)


def kernel(x, pre_norm_w, w_in, gmlp_v_norm_w, gmlp_v_norm_b, gmlp_ws, gmlp_bs, gmlp_norm_w,
           conv_w, conv_b, dt_bias, a_log, d_skip, ssd_norm_w, w_out, post_norm_w):
    depth = w_in.shape[0]
    w_main = w_in.astype(BF16)
    w_dt = jnp.tile(w_in[:, :, MAIN_COLS:], (1, 1, HEAD_REPLICAS)).astype(BF16)
    w_out_bf = w_out.astype(BF16)
    h = x
    for i in range(depth):
        bsf = jnp.repeat(gmlp_bs[i].T, GMLP_HEAD_DIM, axis=1)
        h = _layer(
            i,
            h,
            pre_norm_w[i][None, :],
            w_main,
            w_dt,
            gmlp_v_norm_w[i][None, :],
            gmlp_v_norm_b[i][None, :],
            gmlp_ws[i],
            bsf,
            gmlp_norm_w[i][None, :],
            conv_w[i],
            conv_b[i][None, :],
            jnp.tile(dt_bias[i], HEAD_REPLICAS)[None, :],
            jnp.tile(a_log[i], HEAD_REPLICAS)[None, :],
            jnp.repeat(d_skip[i], SSD_HEAD_DIM)[None, :],
            ssd_norm_w[i][None, :],
            w_out_bf,
            post_norm_w[i][None, :],
        )
    return h
```

```python
import functools

import jax
import jax.numpy as jnp
from jax import lax
from jax.experimental import pallas as pl
from jax.experimental.pallas import tpu as pltpu

D_MODEL = 1024
CHUNK = 128
GMLP_WIDTH = 1024
GMLP_HEADS = 8
GMLP_HEAD_DIM = 128
SSD_WIDTH = 1024
SSD_HEAD_DIM = 64
SSD_HEADS = 16
SSD_GROUPS = 4
HEADS_PER_GROUP = 4
GROUP_WIDTH = SSD_WIDTH // SSD_GROUPS
D_STATE = 128
CONV_WIDTH = 4
CONV_DIM = SSD_WIDTH + 2 * SSD_GROUPS * D_STATE
MIX_WIDTH = GMLP_WIDTH + SSD_WIDTH
EPS = 1e-6

OFF_U = 0
OFF_V = GMLP_WIDTH
OFF_ZG = 2 * GMLP_WIDTH
OFF_Z = 3 * GMLP_WIDTH
OFF_XBC = 3 * GMLP_WIDTH + SSD_WIDTH
MAIN_COLS = OFF_XBC + CONV_DIM
OFF_B = SSD_WIDTH
OFF_C = SSD_WIDTH + SSD_GROUPS * D_STATE

LANES = 128
SUBLANES = 8
MXU_DIM = 256
HEAD_REPLICAS = LANES // SSD_HEADS
PAD_ROWS = SUBLANES
ROW_TILE = 256
PROJ_COL_BLOCK = MXU_DIM
OUT_K_BLOCK = 2 * MXU_DIM

F32 = jnp.float32
BF16 = jnp.bfloat16


def _split3(x):
    hi = x.astype(BF16).astype(F32)
    r1 = x - hi
    mid = r1.astype(BF16).astype(F32)
    lo = r1 - mid
    return hi, mid, lo


def _layer_kernel(x_ref, xres_ref, pre_w_ref, w_main_ref, w_dt_ref, vw_ref, vb_ref, ws_ref, bsf_ref,
                  gw_ref, cw_ref, cb_ref, dtb_ref, alog_ref, dsk_ref, nw_ref, w_out_ref,
                  post_w_ref, out_ref,
                  proj0_ref, proj1_ref, xbc0_ref, xbc1_ref, dtraw0_ref, dtraw1_ref,
                  act_ref, ycat_ref, state_ref, wsm_ref, mixed_ref,
                  *, row_tile, tiles_per_seq):
    n_chunks = row_tile // CHUNK
    s = pl.program_id(0)
    t_prev = (s + tiles_per_seq - 1) % tiles_per_seq

    row_i = lax.broadcasted_iota(jnp.int32, (CHUNK, CHUNK), 0)
    col_i = lax.broadcasted_iota(jnp.int32, (CHUNK, CHUNK), 1)
    causal = row_i >= col_i

    @pl.when(s == 0)
    def _first_step():
        proj1_ref[...] = jnp.zeros_like(proj1_ref)
        xbc1_ref[...] = jnp.zeros_like(xbc1_ref)
        dtraw1_ref[...] = jnp.zeros_like(dtraw1_ref)
        state_ref[...] = jnp.zeros_like(state_ref)
        for hd in range(GMLP_HEADS):
            wsm_ref[hd] = jnp.where(causal, ws_ref[hd], 0.0).astype(BF16)

    def run(proj_a_ref, xbc_a_ref, dtraw_a_ref, proj_b_ref, xbc_b_ref, dtraw_b_ref):
        @pl.when(t_prev == 0)
        def _start_of_sequence():
            state_ref[...] = jnp.zeros_like(state_ref)
            xbc_b_ref[:, 0:PAD_ROWS, :] = jnp.zeros((CONV_DIM // LANES, PAD_ROWS, LANES), F32)

        x = x_ref[0]
        h = x * lax.rsqrt(jnp.mean(x * x, axis=-1, keepdims=True) + EPS) * pre_w_ref[...]
        hb = h.astype(BF16)

        def proj_task(j):
            cols = slice(j * PROJ_COL_BLOCK, (j + 1) * PROJ_COL_BLOCK)
            d = jnp.dot(hb, w_main_ref[:, cols], preferred_element_type=F32)
            if cols.start < OFF_XBC:
                proj_a_ref[:, cols] = d
            else:
                for i in range(PROJ_COL_BLOCK // LANES):
                    blk = (cols.start - OFF_XBC) // LANES + i
                    xbc_a_ref[blk, PAD_ROWS:PAD_ROWS + row_tile, :] = d[:, i * LANES:(i + 1) * LANES]

        def dt_task():
            dtraw_a_ref[...] = jnp.dot(hb, w_dt_ref[...], preferred_element_type=F32)

        a_tasks = [dt_task] + [functools.partial(proj_task, j) for j in range(MAIN_COLS // PROJ_COL_BLOCK)]

        def emit_a(n=1):
            for _ in range(n):
                if a_tasks:
                    a_tasks.pop(0)()

        def out_task(kb):
            ks = slice(kb * OUT_K_BLOCK, (kb + 1) * OUT_K_BLOCK)
            d = jnp.dot(ycat_ref[:, ks], w_out_ref[ks, :], preferred_element_type=F32)
            if kb == 0:
                mixed_ref[...] = d
            else:
                mixed_ref[...] += d

        tril_bf = jnp.where(causal, 1.0, 0.0).astype(BF16)
        lane_head_in_group = lax.broadcasted_iota(jnp.int32, (CHUNK, GROUP_WIDTH), 1) // SSD_HEAD_DIM
        lane_low_half = col_i < SSD_HEAD_DIM
        a_row = -jnp.exp(alog_ref[...])

        emit_a(2)
        vnb = []
        for c in range(n_chunks):
            rows = slice(c * CHUNK, (c + 1) * CHUNK)
            v = proj_b_ref[rows, OFF_V:OFF_V + GMLP_WIDTH]
            mu = jnp.mean(v, axis=-1, keepdims=True)
            vc = v - mu
            vn = vc * lax.rsqrt(jnp.mean(vc * vc, axis=-1, keepdims=True) + EPS)
            vnb.append((vn * vw_ref[...] + vb_ref[...]).astype(BF16))
            emit_a()
        heads_per_block = OUT_K_BLOCK // GMLP_HEAD_DIM
        for kb in range(GMLP_WIDTH // OUT_K_BLOCK):
            for c in range(n_chunks):
                rows = slice(c * CHUNK, (c + 1) * CHUNK)
                for hd in range(kb * heads_per_block, (kb + 1) * heads_per_block):
                    cs_ = slice(hd * GMLP_HEAD_DIM, (hd + 1) * GMLP_HEAD_DIM)
                    sg = jnp.dot(wsm_ref[hd], vnb[c][:, cs_], preferred_element_type=F32) + bsf_ref[:, cs_]
                    u = proj_b_ref[rows, OFF_U + hd * GMLP_HEAD_DIM:OFF_U + (hd + 1) * GMLP_HEAD_DIM]
                    zg = proj_b_ref[rows, OFF_ZG + hd * GMLP_HEAD_DIM:OFF_ZG + (hd + 1) * GMLP_HEAD_DIM]
                    y = u * sg * jax.nn.silu(zg)
                    yn = y * lax.rsqrt(jnp.mean(y * y, axis=-1, keepdims=True) + EPS) * gw_ref[:, cs_]
                    ycat_ref[rows, cs_] = yn.astype(BF16)
            out_task(kb)
            emit_a()

        groups_per_block = OUT_K_BLOCK // GROUP_WIDTH
        for c in range(n_chunks):
            rows = slice(c * CHUNK, (c + 1) * CHUNK)
            r0 = PAD_ROWS + c * CHUNK

            for j in range(CONV_DIM // LANES):
                cc = slice(j * LANES, (j + 1) * LANES)
                acc = cb_ref[:, cc] + cw_ref[3:4, cc] * xbc_b_ref[j, r0:r0 + CHUNK, :]
                acc = acc + cw_ref[2:3, cc] * xbc_b_ref[j, r0 - 1:r0 - 1 + CHUNK, :]
                acc = acc + cw_ref[1:2, cc] * xbc_b_ref[j, r0 - 2:r0 - 2 + CHUNK, :]
                acc = acc + cw_ref[0:1, cc] * xbc_b_ref[j, r0 - 3:r0 - 3 + CHUNK, :]
                act_ref[:, cc] = jax.nn.silu(acc)
                if j % 4 == 3:
                    emit_a()

            dt = jax.nn.softplus(dtraw_b_ref[rows, :] + dtb_ref[...])
            da = dt * a_row
            da_pieces = jnp.concatenate(_split3(da), axis=1).astype(BF16)
            cs3 = jnp.dot(tril_bf, da_pieces, preferred_element_type=F32)
            cs = cs3[:, 0:LANES] + cs3[:, LANES:2 * LANES] + cs3[:, 2 * LANES:3 * LANES]
            cs_t = cs.T
            dt_t = dt.T
            cs_last = cs[CHUNK - 1:CHUNK, :]
            ecs = jnp.exp(cs)
            dd = dt * jnp.exp(cs_last - cs)

            def to_channels(q, g):
                blocks = []
                for half in range(GROUP_WIDTH // LANES):
                    h0 = g * HEADS_PER_GROUP + 2 * half
                    lo = jnp.broadcast_to(q[:, h0:h0 + 1], (CHUNK, LANES))
                    hi = jnp.broadcast_to(q[:, h0 + 1:h0 + 2], (CHUNK, LANES))
                    blocks.append(jnp.where(lane_low_half, lo, hi))
                return jnp.concatenate(blocks, axis=1)
            emit_a()

            for g in range(SSD_GROUPS):
                gc = slice(g * GROUP_WIDTH, (g + 1) * GROUP_WIDTH)
                xs_g = act_ref[:, gc]
                xs_gb = xs_g.astype(BF16)
                b_g = act_ref[:, OFF_B + g * D_STATE:OFF_B + (g + 1) * D_STATE].astype(BF16)
                c_g = act_ref[:, OFF_C + g * D_STATE:OFF_C + (g + 1) * D_STATE].astype(BF16)
                cb_g = lax.dot_general(c_g, b_g, (((1,), (1,)), ((), ())), preferred_element_type=F32)
                g_blocks = []
                x_blocks = []
                for r in range(HEADS_PER_GROUP):
                    hh = g * HEADS_PER_GROUP + r
                    seg = cs[:, hh:hh + 1] - cs_t[hh:hh + 1, :]
                    dec = jnp.exp(jnp.where(causal, seg, -jnp.inf))
                    g_blocks.append((cb_g * dec * dt_t[hh:hh + 1, :]).astype(BF16))
                    x_blocks.append(jnp.where(lane_head_in_group == r, xs_gb, jnp.zeros_like(xs_gb)))
                y_diag = jnp.dot(jnp.concatenate(g_blocks, axis=1), jnp.concatenate(x_blocks, axis=0),
                                 preferred_element_type=F32)
                st = state_ref[g]
                ecs_g = to_channels(ecs, g)
                y_off = jnp.dot(c_g, st.astype(BF16), preferred_element_type=F32) * ecs_g
                xdd_g = (xs_g * to_channels(dd, g)).astype(BF16)
                st_new = lax.dot_general(b_g, xdd_g, (((0,), (0,)), ((), ())), preferred_element_type=F32)
                state_ref[g] = st * ecs_g[CHUNK - 1:CHUNK, :] + st_new
                y = y_diag + y_off + xs_g * dsk_ref[:, gc]
                z = proj_b_ref[rows, OFF_Z + g * GROUP_WIDTH:OFF_Z + (g + 1) * GROUP_WIDTH]
                yz = y * jax.nn.silu(z)
                yn = yz * lax.rsqrt(jnp.mean(yz * yz, axis=-1, keepdims=True) + EPS) * nw_ref[:, gc]
                ycat_ref[rows, GMLP_WIDTH + g * GROUP_WIDTH:GMLP_WIDTH + (g + 1) * GROUP_WIDTH] = yn.astype(BF16)
                if c == n_chunks - 1 and g % groups_per_block == groups_per_block - 1:
                    out_task(GMLP_WIDTH // OUT_K_BLOCK + g // groups_per_block)
                emit_a()

        emit_a(len(a_tasks))

        xbc_a_ref[:, 0:PAD_ROWS, :] = xbc_b_ref[:, row_tile:row_tile + PAD_ROWS, :]

        mx = mixed_ref[...]
        mn = mx * lax.rsqrt(jnp.mean(mx * mx, axis=-1, keepdims=True) + EPS) * post_w_ref[...]
        out_ref[0] = xres_ref[0] + mn

    @pl.when(s % 2 == 0)
    def _even_step():
        run(proj0_ref, xbc0_ref, dtraw0_ref, proj1_ref, xbc1_ref, dtraw1_ref)

    @pl.when(s % 2 == 1)
    def _odd_step():
        run(proj1_ref, xbc1_ref, dtraw1_ref, proj0_ref, xbc0_ref, dtraw0_ref)


def _vmem_limit_bytes(row_tile):
    weights = (D_MODEL * MAIN_COLS + D_MODEL * LANES + MIX_WIDTH * D_MODEL) * 2
    small = (GMLP_HEADS * CHUNK * CHUNK + CHUNK * GMLP_WIDTH) * 4 + 64 * 1024
    io = 3 * 2 * row_tile * D_MODEL * 4
    per_tile = (row_tile * OFF_XBC + (PAD_ROWS + row_tile) * CONV_DIM + row_tile * LANES) * 4
    scratch = (2 * per_tile + CHUNK * CONV_DIM * 4 + row_tile * MIX_WIDTH * 2
               + SSD_GROUPS * D_STATE * GROUP_WIDTH * 4 + GMLP_HEADS * CHUNK * CHUNK * 2
               + row_tile * D_MODEL * 4)
    temporaries = 8 * 1024 * 1024
    return weights + small + io + scratch + temporaries


def _layer(layer, x, pre_w, w_main, w_dt, vw, vb, ws, bsf, gw, cw, cb, dtb, alog, dsk, nw, w_out, post_w,
           *, row_tile=ROW_TILE):
    bsz, seq, _ = x.shape
    assert seq % row_tile == 0 and row_tile % CHUNK == 0
    tiles_per_seq = seq // row_tile
    n_tiles = bsz * tiles_per_seq

    def const(shape):
        nd = len(shape)
        return pl.BlockSpec(shape, lambda s: (0,) * nd, pipeline_mode=pl.Buffered(1))

    def of_layer(shape):
        return pl.BlockSpec((None,) + shape, lambda s: (layer, 0, 0), pipeline_mode=pl.Buffered(1))

    def tile_index(i):
        return (i // tiles_per_seq, i % tiles_per_seq, 0)

    x_spec = pl.BlockSpec((1, row_tile, D_MODEL), lambda s: tile_index(jnp.minimum(s, n_tiles - 1)))
    out_spec = pl.BlockSpec((1, row_tile, D_MODEL), lambda s: tile_index(jnp.maximum(s - 1, 0)))
    in_specs = [
        x_spec,
        out_spec,
        const((1, D_MODEL)),
        of_layer((D_MODEL, MAIN_COLS)),
        of_layer((D_MODEL, LANES)),
        const((1, GMLP_WIDTH)),
        const((1, GMLP_WIDTH)),
        const((GMLP_HEADS, CHUNK, CHUNK)),
        const((CHUNK, GMLP_WIDTH)),
        const((1, GMLP_WIDTH)),
        const((CONV_WIDTH, CONV_DIM)),
        const((1, CONV_DIM)),
        const((1, LANES)),
        const((1, LANES)),
        const((1, SSD_WIDTH)),
        const((1, SSD_WIDTH)),
        of_layer((MIX_WIDTH, D_MODEL)),
        const((1, D_MODEL)),
    ]
    proj_buf = pltpu.VMEM((row_tile, OFF_XBC), F32)
    xbc_buf = pltpu.VMEM((CONV_DIM // LANES, PAD_ROWS + row_tile, LANES), F32)
    dtraw_buf = pltpu.VMEM((row_tile, LANES), F32)
    scratch = [
        proj_buf, proj_buf, xbc_buf, xbc_buf, dtraw_buf, dtraw_buf,
        pltpu.VMEM((CHUNK, CONV_DIM), F32),
        pltpu.VMEM((row_tile, MIX_WIDTH), BF16),
        pltpu.VMEM((SSD_GROUPS, D_STATE, GROUP_WIDTH), F32),
        pltpu.VMEM((GMLP_HEADS, CHUNK, CHUNK), BF16),
        pltpu.VMEM((row_tile, D_MODEL), F32),
    ]
    return pl.pallas_call(
        functools.partial(_layer_kernel, row_tile=row_tile, tiles_per_seq=tiles_per_seq),
        grid=(n_tiles + 1,),
        in_specs=in_specs,
        out_specs=out_spec,
        out_shape=jax.ShapeDtypeStruct(x.shape, x.dtype),
        scratch_shapes=scratch,
        compiler_params=pltpu.CompilerParams(
            dimension_semantics=("arbitrary",),
            vmem_limit_bytes=_vmem_limit_bytes(row_tile)),
        name="hybrid_layer",
    )(x, x, pre_w, w_main, w_dt, vw, vb, ws, bsf, gw, cw, cb, dtb, alog, dsk, nw, w_out, post_w)


def kernel(x, pre_norm_w, w_in, gmlp_v_norm_w, gmlp_v_norm_b, gmlp_ws, gmlp_bs, gmlp_norm_w,
           conv_w, conv_b, dt_bias, a_log, d_skip, ssd_norm_w, w_out, post_norm_w):
    depth = w_in.shape[0]
    w_main = w_in.astype(BF16)
    w_dt = jnp.tile(w_in[:, :, MAIN_COLS:], (1, 1, HEAD_REPLICAS)).astype(BF16)
    w_out_bf = w_out.astype(BF16)
    h = x
    for i in range(depth):
        bsf = jnp.repeat(gmlp_bs[i].T, GMLP_HEAD_DIM, axis=1)
        h = _layer(
            i,
            h,
            pre_norm_w[i][None, :],
            w_main,
            w_dt,
            gmlp_v_norm_w[i][None, :],
            gmlp_v_norm_b[i][None, :],
            gmlp_ws[i],
            bsf,
            gmlp_norm_w[i][None, :],
            conv_w[i],
            conv_b[i][None, :],
            jnp.tile(dt_bias[i], HEAD_REPLICAS)[None, :],
            jnp.tile(a_log[i], HEAD_REPLICAS)[None, :],
            jnp.repeat(d_skip[i], SSD_HEAD_DIM)[None, :],
            ssd_norm_w[i][None, :],
            w_out_bf,
            post_norm_w[i][None, :],
        )
    return h
```

```python
import functools

import jax
import jax.numpy as jnp
from jax import lax
from jax.experimental import pallas as pl
from jax.experimental.pallas import tpu as pltpu

D_MODEL = 1024
CHUNK = 128
GMLP_WIDTH = 1024
GMLP_HEADS = 8
GMLP_HEAD_DIM = 128
SSD_WIDTH = 1024
SSD_HEAD_DIM = 64
SSD_HEADS = 16
SSD_GROUPS = 4
HEADS_PER_GROUP = 4
GROUP_WIDTH = SSD_WIDTH // SSD_GROUPS
D_STATE = 128
CONV_WIDTH = 4
CONV_DIM = SSD_WIDTH + 2 * SSD_GROUPS * D_STATE
MIX_WIDTH = GMLP_WIDTH + SSD_WIDTH
EPS = 1e-6

OFF_U = 0
OFF_V = GMLP_WIDTH
OFF_ZG = 2 * GMLP_WIDTH
OFF_Z = 3 * GMLP_WIDTH
OFF_XBC = 3 * GMLP_WIDTH + SSD_WIDTH
MAIN_COLS = OFF_XBC + CONV_DIM
OFF_B = SSD_WIDTH
OFF_C = SSD_WIDTH + SSD_GROUPS * D_STATE

LANES = 128
SUBLANES = 8
MXU_DIM = 256
HEAD_REPLICAS = LANES // SSD_HEADS
PAD_ROWS = SUBLANES
ROW_TILE = 256
PROJ_COL_BLOCK = MXU_DIM
OUT_K_BLOCK = 2 * MXU_DIM
CAST_BLOCK_ROWS = 256

F32 = jnp.float32
BF16 = jnp.bfloat16


def _split3(x):
    hi = x.astype(BF16).astype(F32)
    r1 = x - hi
    mid = r1.astype(BF16).astype(F32)
    lo = r1 - mid
    return hi, mid, lo


def _layer_kernel(x_ref, xres_ref, pre_w_ref, w_main_ref, w_dt_ref, vw_ref, vb_ref, ws_ref, bsf_ref,
                  gw_ref, cw_ref, cb_ref, dtb_ref, alog_ref, dsk_ref, nw_ref, w_out_ref,
                  post_w_ref, out_ref,
                  proj0_ref, proj1_ref, xbc0_ref, xbc1_ref, dtraw0_ref, dtraw1_ref,
                  act_ref, ycat_ref, state_ref, wsm_ref, mixed_ref,
                  *, row_tile, tiles_per_seq):
    n_chunks = row_tile // CHUNK
    s = pl.program_id(0)
    t_prev = (s + tiles_per_seq - 1) % tiles_per_seq

    row_i = lax.broadcasted_iota(jnp.int32, (CHUNK, CHUNK), 0)
    col_i = lax.broadcasted_iota(jnp.int32, (CHUNK, CHUNK), 1)
    causal = row_i >= col_i

    @pl.when(s == 0)
    def _first_step():
        proj1_ref[...] = jnp.zeros_like(proj1_ref)
        xbc1_ref[...] = jnp.zeros_like(xbc1_ref)
        dtraw1_ref[...] = jnp.zeros_like(dtraw1_ref)
        state_ref[...] = jnp.zeros_like(state_ref)
        for hd in range(GMLP_HEADS):
            wsm_ref[hd] = jnp.where(causal, ws_ref[hd], 0.0).astype(BF16)

    def run(proj_a_ref, xbc_a_ref, dtraw_a_ref, proj_b_ref, xbc_b_ref, dtraw_b_ref):
        @pl.when(t_prev == 0)
        def _start_of_sequence():
            state_ref[...] = jnp.zeros_like(state_ref)
            xbc_b_ref[:, 0:PAD_ROWS, :] = jnp.zeros((CONV_DIM // LANES, PAD_ROWS, LANES), F32)

        x = x_ref[0]
        h = x * lax.rsqrt(jnp.mean(x * x, axis=-1, keepdims=True) + EPS) * pre_w_ref[...]
        hb = h.astype(BF16)

        def proj_task(j):
            cols = slice(j * PROJ_COL_BLOCK, (j + 1) * PROJ_COL_BLOCK)
            d = jnp.dot(hb, w_main_ref[:, cols], preferred_element_type=F32)
            if cols.start < OFF_XBC:
                proj_a_ref[:, cols] = d
            else:
                for i in range(PROJ_COL_BLOCK // LANES):
                    blk = (cols.start - OFF_XBC) // LANES + i
                    xbc_a_ref[blk, PAD_ROWS:PAD_ROWS + row_tile, :] = d[:, i * LANES:(i + 1) * LANES]

        def dt_task():
            dtraw_a_ref[...] = jnp.dot(hb, w_dt_ref[...], preferred_element_type=F32)

        a_tasks = [dt_task] + [functools.partial(proj_task, j) for j in range(MAIN_COLS // PROJ_COL_BLOCK)]

        def emit_a(n=1):
            for _ in range(n):
                if a_tasks:
                    a_tasks.pop(0)()

        def out_task(kb):
            ks = slice(kb * OUT_K_BLOCK, (kb + 1) * OUT_K_BLOCK)
            d = jnp.dot(ycat_ref[:, ks], w_out_ref[ks, :], preferred_element_type=F32)
            if kb == 0:
                mixed_ref[...] = d
            else:
                mixed_ref[...] += d

        tril_bf = jnp.where(causal, 1.0, 0.0).astype(BF16)
        lane_head_in_group = lax.broadcasted_iota(jnp.int32, (CHUNK, GROUP_WIDTH), 1) // SSD_HEAD_DIM
        lane_low_half = col_i < SSD_HEAD_DIM
        a_row = -jnp.exp(alog_ref[...])

        emit_a(2)
        vnb = []
        for c in range(n_chunks):
            rows = slice(c * CHUNK, (c + 1) * CHUNK)
            v = proj_b_ref[rows, OFF_V:OFF_V + GMLP_WIDTH]
            mu = jnp.mean(v, axis=-1, keepdims=True)
            vc = v - mu
            vn = vc * lax.rsqrt(jnp.mean(vc * vc, axis=-1, keepdims=True) + EPS)
            vnb.append((vn * vw_ref[...] + vb_ref[...]).astype(BF16))
            emit_a()
        heads_per_block = OUT_K_BLOCK // GMLP_HEAD_DIM
        for kb in range(GMLP_WIDTH // OUT_K_BLOCK):
            for c in range(n_chunks):
                rows = slice(c * CHUNK, (c + 1) * CHUNK)
                for hd in range(kb * heads_per_block, (kb + 1) * heads_per_block):
                    cs_ = slice(hd * GMLP_HEAD_DIM, (hd + 1) * GMLP_HEAD_DIM)
                    sg = jnp.dot(wsm_ref[hd], vnb[c][:, cs_], preferred_element_type=F32) + bsf_ref[:, cs_]
                    u = proj_b_ref[rows, OFF_U + hd * GMLP_HEAD_DIM:OFF_U + (hd + 1) * GMLP_HEAD_DIM]
                    zg = proj_b_ref[rows, OFF_ZG + hd * GMLP_HEAD_DIM:OFF_ZG + (hd + 1) * GMLP_HEAD_DIM]
                    y = u * sg * jax.nn.silu(zg)
                    yn = y * lax.rsqrt(jnp.mean(y * y, axis=-1, keepdims=True) + EPS) * gw_ref[:, cs_]
                    ycat_ref[rows, cs_] = yn.astype(BF16)
            out_task(kb)
            emit_a()

        groups_per_block = OUT_K_BLOCK // GROUP_WIDTH
        for c in range(n_chunks):
            rows = slice(c * CHUNK, (c + 1) * CHUNK)
            r0 = PAD_ROWS + c * CHUNK

            for j in range(CONV_DIM // LANES):
                cc = slice(j * LANES, (j + 1) * LANES)
                acc = cb_ref[:, cc] + cw_ref[3:4, cc] * xbc_b_ref[j, r0:r0 + CHUNK, :]
                acc = acc + cw_ref[2:3, cc] * xbc_b_ref[j, r0 - 1:r0 - 1 + CHUNK, :]
                acc = acc + cw_ref[1:2, cc] * xbc_b_ref[j, r0 - 2:r0 - 2 + CHUNK, :]
                acc = acc + cw_ref[0:1, cc] * xbc_b_ref[j, r0 - 3:r0 - 3 + CHUNK, :]
                act_ref[:, cc] = jax.nn.silu(acc)
                if j % 4 == 3:
                    emit_a()

            dt = jax.nn.softplus(dtraw_b_ref[rows, :] + dtb_ref[...])
            da = dt * a_row
            da_pieces = jnp.concatenate(_split3(da), axis=1).astype(BF16)
            cs3 = jnp.dot(tril_bf, da_pieces, preferred_element_type=F32)
            cs = cs3[:, 0:LANES] + cs3[:, LANES:2 * LANES] + cs3[:, 2 * LANES:3 * LANES]
            cs_t = cs.T
            dt_t = dt.T
            cs_last = cs[CHUNK - 1:CHUNK, :]
            ecs = jnp.exp(cs)
            dd = dt * jnp.exp(cs_last - cs)

            def to_channels(q, g):
                blocks = []
                for half in range(GROUP_WIDTH // LANES):
                    h0 = g * HEADS_PER_GROUP + 2 * half
                    lo = jnp.broadcast_to(q[:, h0:h0 + 1], (CHUNK, LANES))
                    hi = jnp.broadcast_to(q[:, h0 + 1:h0 + 2], (CHUNK, LANES))
                    blocks.append(jnp.where(lane_low_half, lo, hi))
                return jnp.concatenate(blocks, axis=1)
            emit_a()

            for g in range(SSD_GROUPS):
                gc = slice(g * GROUP_WIDTH, (g + 1) * GROUP_WIDTH)
                xs_g = act_ref[:, gc]
                xs_gb = xs_g.astype(BF16)
                b_g = act_ref[:, OFF_B + g * D_STATE:OFF_B + (g + 1) * D_STATE].astype(BF16)
                c_g = act_ref[:, OFF_C + g * D_STATE:OFF_C + (g + 1) * D_STATE].astype(BF16)
                cb_g = lax.dot_general(c_g, b_g, (((1,), (1,)), ((), ())), preferred_element_type=F32)
                g_blocks = []
                x_blocks = []
                for r in range(HEADS_PER_GROUP):
                    hh = g * HEADS_PER_GROUP + r
                    seg = cs[:, hh:hh + 1] - cs_t[hh:hh + 1, :]
                    dec = jnp.exp(jnp.where(causal, seg, -jnp.inf))
                    g_blocks.append((cb_g * dec * dt_t[hh:hh + 1, :]).astype(BF16))
                    x_blocks.append(jnp.where(lane_head_in_group == r, xs_gb, jnp.zeros_like(xs_gb)))
                y_diag = jnp.dot(jnp.concatenate(g_blocks, axis=1), jnp.concatenate(x_blocks, axis=0),
                                 preferred_element_type=F32)
                st = state_ref[g]
                ecs_g = to_channels(ecs, g)
                y_off = jnp.dot(c_g, st.astype(BF16), preferred_element_type=F32) * ecs_g
                xdd_g = (xs_g * to_channels(dd, g)).astype(BF16)
                st_new = lax.dot_general(b_g, xdd_g, (((0,), (0,)), ((), ())), preferred_element_type=F32)
                state_ref[g] = st * ecs_g[CHUNK - 1:CHUNK, :] + st_new
                y = y_diag + y_off + xs_g * dsk_ref[:, gc]
                z = proj_b_ref[rows, OFF_Z + g * GROUP_WIDTH:OFF_Z + (g + 1) * GROUP_WIDTH]
                yz = y * jax.nn.silu(z)
                yn = yz * lax.rsqrt(jnp.mean(yz * yz, axis=-1, keepdims=True) + EPS) * nw_ref[:, gc]
                ycat_ref[rows, GMLP_WIDTH + g * GROUP_WIDTH:GMLP_WIDTH + (g + 1) * GROUP_WIDTH] = yn.astype(BF16)
                if c == n_chunks - 1 and g % groups_per_block == groups_per_block - 1:
                    out_task(GMLP_WIDTH // OUT_K_BLOCK + g // groups_per_block)
                emit_a()

        emit_a(len(a_tasks))

        xbc_a_ref[:, 0:PAD_ROWS, :] = xbc_b_ref[:, row_tile:row_tile + PAD_ROWS, :]

        mx = mixed_ref[...]
        mn = mx * lax.rsqrt(jnp.mean(mx * mx, axis=-1, keepdims=True) + EPS) * post_w_ref[...]
        out_ref[0] = xres_ref[0] + mn

    @pl.when(s % 2 == 0)
    def _even_step():
        run(proj0_ref, xbc0_ref, dtraw0_ref, proj1_ref, xbc1_ref, dtraw1_ref)

    @pl.when(s % 2 == 1)
    def _odd_step():
        run(proj1_ref, xbc1_ref, dtraw1_ref, proj0_ref, xbc0_ref, dtraw0_ref)


def _cast_kernel(src_ref, dst_ref):
    dst_ref[...] = src_ref[...].astype(BF16)


def _to_bf16(w, block_rows):
    depth, rows, cols = w.shape
    assert rows % block_rows == 0
    spec = pl.BlockSpec((None, block_rows, cols), lambda l, r: (l, r, 0))
    block_bytes = block_rows * cols * (4 + 2)
    return pl.pallas_call(
        _cast_kernel,
        grid=(depth, rows // block_rows),
        in_specs=[spec],
        out_specs=spec,
        out_shape=jax.ShapeDtypeStruct(w.shape, BF16),
        compiler_params=pltpu.CompilerParams(
            dimension_semantics=("arbitrary", "arbitrary"),
            vmem_limit_bytes=2 * block_bytes + 4 * 1024 * 1024),
        name="weights_to_bf16",
    )(w)


def _vmem_limit_bytes(row_tile):
    weights = (D_MODEL * MAIN_COLS + D_MODEL * LANES + MIX_WIDTH * D_MODEL) * 2
    small = (GMLP_HEADS * CHUNK * CHUNK + CHUNK * GMLP_WIDTH) * 4 + 64 * 1024
    io = 3 * 2 * row_tile * D_MODEL * 4
    per_tile = (row_tile * OFF_XBC + (PAD_ROWS + row_tile) * CONV_DIM + row_tile * LANES) * 4
    scratch = (2 * per_tile + CHUNK * CONV_DIM * 4 + row_tile * MIX_WIDTH * 2
               + SSD_GROUPS * D_STATE * GROUP_WIDTH * 4 + GMLP_HEADS * CHUNK * CHUNK * 2
               + row_tile * D_MODEL * 4)
    temporaries = 8 * 1024 * 1024
    return weights + small + io + scratch + temporaries


def _layer(layer, x, pre_w, w_main, w_dt, vw, vb, ws, bsf, gw, cw, cb, dtb, alog, dsk, nw, w_out, post_w,
           *, row_tile=ROW_TILE):
    bsz, seq, _ = x.shape
    assert seq % row_tile == 0 and row_tile % CHUNK == 0
    tiles_per_seq = seq // row_tile
    n_tiles = bsz * tiles_per_seq

    def const(shape):
        nd = len(shape)
        return pl.BlockSpec(shape, lambda s: (0,) * nd, pipeline_mode=pl.Buffered(1))

    def of_layer(shape):
        return pl.BlockSpec((None,) + shape, lambda s: (layer, 0, 0), pipeline_mode=pl.Buffered(1))

    def tile_index(i):
        return (i // tiles_per_seq, i % tiles_per_seq, 0)

    x_spec = pl.BlockSpec((1, row_tile, D_MODEL), lambda s: tile_index(jnp.minimum(s, n_tiles - 1)))
    out_spec = pl.BlockSpec((1, row_tile, D_MODEL), lambda s: tile_index(jnp.maximum(s - 1, 0)))
    in_specs = [
        x_spec,
        out_spec,
        const((1, D_MODEL)),
        of_layer((D_MODEL, MAIN_COLS)),
        of_layer((D_MODEL, LANES)),
        const((1, GMLP_WIDTH)),
        const((1, GMLP_WIDTH)),
        const((GMLP_HEADS, CHUNK, CHUNK)),
        const((CHUNK, GMLP_WIDTH)),
        const((1, GMLP_WIDTH)),
        const((CONV_WIDTH, CONV_DIM)),
        const((1, CONV_DIM)),
        const((1, LANES)),
        const((1, LANES)),
        const((1, SSD_WIDTH)),
        const((1, SSD_WIDTH)),
        of_layer((MIX_WIDTH, D_MODEL)),
        const((1, D_MODEL)),
    ]
    proj_buf = pltpu.VMEM((row_tile, OFF_XBC), F32)
    xbc_buf = pltpu.VMEM((CONV_DIM // LANES, PAD_ROWS + row_tile, LANES), F32)
    dtraw_buf = pltpu.VMEM((row_tile, LANES), F32)
    scratch = [
        proj_buf, proj_buf, xbc_buf, xbc_buf, dtraw_buf, dtraw_buf,
        pltpu.VMEM((CHUNK, CONV_DIM), F32),
        pltpu.VMEM((row_tile, MIX_WIDTH), BF16),
        pltpu.VMEM((SSD_GROUPS, D_STATE, GROUP_WIDTH), F32),
        pltpu.VMEM((GMLP_HEADS, CHUNK, CHUNK), BF16),
        pltpu.VMEM((row_tile, D_MODEL), F32),
    ]
    return pl.pallas_call(
        functools.partial(_layer_kernel, row_tile=row_tile, tiles_per_seq=tiles_per_seq),
        grid=(n_tiles + 1,),
        in_specs=in_specs,
        out_specs=out_spec,
        out_shape=jax.ShapeDtypeStruct(x.shape, x.dtype),
        scratch_shapes=scratch,
        compiler_params=pltpu.CompilerParams(
            dimension_semantics=("arbitrary",),
            vmem_limit_bytes=_vmem_limit_bytes(row_tile)),
        name="hybrid_layer",
    )(x, x, pre_w, w_main, w_dt, vw, vb, ws, bsf, gw, cw, cb, dtb, alog, dsk, nw, w_out, post_w)


def kernel(x, pre_norm_w, w_in, gmlp_v_norm_w, gmlp_v_norm_b, gmlp_ws, gmlp_bs, gmlp_norm_w,
           conv_w, conv_b, dt_bias, a_log, d_skip, ssd_norm_w, w_out, post_norm_w):
    depth = w_in.shape[0]
    w_main = _to_bf16(w_in, CAST_BLOCK_ROWS)
    w_dt = jnp.tile(w_in[:, :, MAIN_COLS:], (1, 1, HEAD_REPLICAS)).astype(BF16)
    w_out_bf = w_out.astype(BF16)
    h = x
    for i in range(depth):
        bsf = jnp.repeat(gmlp_bs[i].T, GMLP_HEAD_DIM, axis=1)
        h = _layer(
            i,
            h,
            pre_norm_w[i][None, :],
            w_main,
            w_dt,
            gmlp_v_norm_w[i][None, :],
            gmlp_v_norm_b[i][None, :],
            gmlp_ws[i],
            bsf,
            gmlp_norm_w[i][None, :],
            conv_w[i],
            conv_b[i][None, :],
            jnp.tile(dt_bias[i], HEAD_REPLICAS)[None, :],
            jnp.tile(a_log[i], HEAD_REPLICAS)[None, :],
            jnp.repeat(d_skip[i], SSD_HEAD_DIM)[None, :],
            ssd_norm_w[i][None, :],
            w_out_bf,
            post_norm_w[i][None, :],
        )
    return h
```

```python
import functools

import jax
import jax.numpy as jnp
from jax import lax
from jax.experimental import pallas as pl
from jax.experimental.pallas import tpu as pltpu

D_MODEL = 1024
CHUNK = 128
GMLP_WIDTH = 1024
GMLP_HEADS = 8
GMLP_HEAD_DIM = 128
SSD_WIDTH = 1024
SSD_HEAD_DIM = 64
SSD_HEADS = 16
SSD_GROUPS = 4
HEADS_PER_GROUP = 4
GROUP_WIDTH = SSD_WIDTH // SSD_GROUPS
D_STATE = 128
CONV_WIDTH = 4
CONV_DIM = SSD_WIDTH + 2 * SSD_GROUPS * D_STATE
MIX_WIDTH = GMLP_WIDTH + SSD_WIDTH
EPS = 1e-6

OFF_U = 0
OFF_V = GMLP_WIDTH
OFF_ZG = 2 * GMLP_WIDTH
OFF_Z = 3 * GMLP_WIDTH
OFF_XBC = 3 * GMLP_WIDTH + SSD_WIDTH
MAIN_COLS = OFF_XBC + CONV_DIM
OFF_B = SSD_WIDTH
OFF_C = SSD_WIDTH + SSD_GROUPS * D_STATE

LANES = 128
SUBLANES = 8
MXU_DIM = 256
HEAD_REPLICAS = LANES // SSD_HEADS
PAD_ROWS = SUBLANES
ROW_TILE = 256
PROJ_COL_BLOCK = MXU_DIM
OUT_K_BLOCK = 2 * MXU_DIM
N_SPLIT = 3
W_CHUNK = 2 * MXU_DIM

F32 = jnp.float32
BF16 = jnp.bfloat16


def _split3(x):
    hi = x.astype(BF16).astype(F32)
    r1 = x - hi
    mid = r1.astype(BF16).astype(F32)
    lo = r1 - mid
    return hi, mid, lo


def _load_as_bf16(src_chunk, n, stage_ref, sem, store):
    def copy(c):
        return pltpu.make_async_copy(src_chunk(c), stage_ref.at[c % 2], sem.at[c % 2])

    copy(0).start()
    for c in range(n):
        if c + 1 < n:
            copy(c + 1).start()
        copy(c).wait()
        store(c, stage_ref[c % 2].astype(BF16))


def _layer_kernel(x_ref, pre_w_ref, w_in_hbm, w_dt_ref, vw_ref, vb_ref, ws_ref, bsf_ref,
                  gw_ref, cw_ref, cb_ref, dtb_ref, alog_ref, dsk_ref, nw_ref, w_out_hbm,
                  post_w_ref, out_ref,
                  proj0_ref, proj1_ref, xbc0_ref, xbc1_ref, dtraw0_ref, dtraw1_ref, xkeep0_ref, xkeep1_ref,
                  act_ref, ycat_ref, state_ref, wsm_ref, expand_ref, mixed_ref,
                  w_main_ref, w_out_ref, stage_in_ref, stage_out_ref, sem_in, sem_out,
                  *, layer, row_tile, tiles_per_seq):
    n_chunks = row_tile // CHUNK
    s = pl.program_id(0)
    t_prev = (s + tiles_per_seq - 1) % tiles_per_seq

    row_i = lax.broadcasted_iota(jnp.int32, (CHUNK, CHUNK), 0)
    col_i = lax.broadcasted_iota(jnp.int32, (CHUNK, CHUNK), 1)
    causal = row_i >= col_i

    @pl.when(s == 0)
    def _first_step():
        def store_in(c, v):
            w_main_ref[:, c * W_CHUNK:(c + 1) * W_CHUNK] = v

        def store_out(c, v):
            w_out_ref[c * W_CHUNK:(c + 1) * W_CHUNK, :] = v

        _load_as_bf16(lambda c: w_in_hbm.at[layer, :, pl.ds(c * W_CHUNK, W_CHUNK)],
                      MAIN_COLS // W_CHUNK, stage_in_ref, sem_in, store_in)
        _load_as_bf16(lambda c: w_out_hbm.at[layer, pl.ds(c * W_CHUNK, W_CHUNK), :],
                      MIX_WIDTH // W_CHUNK, stage_out_ref, sem_out, store_out)
        proj1_ref[...] = jnp.zeros_like(proj1_ref)
        xbc1_ref[...] = jnp.zeros_like(xbc1_ref)
        dtraw1_ref[...] = jnp.zeros_like(dtraw1_ref)
        xkeep1_ref[...] = jnp.zeros_like(xkeep1_ref)
        state_ref[...] = jnp.zeros_like(state_ref)
        for hd in range(GMLP_HEADS):
            wsm_ref[hd] = jnp.where(causal, ws_ref[hd], 0.0).astype(BF16)
        ei = lax.broadcasted_iota(jnp.int32, (LANES, 2 * SSD_WIDTH), 0)
        ej = lax.broadcasted_iota(jnp.int32, (LANES, 2 * SSD_WIDTH), 1)
        which = ej // SSD_WIDTH
        head = (ej % SSD_WIDTH) // SSD_HEAD_DIM
        grp = ei // SSD_HEADS
        hit = (ei % SSD_HEADS == head) & (grp >= which * N_SPLIT) & (grp < (which + 1) * N_SPLIT)
        expand_ref[...] = jnp.where(hit, 1.0, 0.0).astype(BF16)

    def run(proj_a_ref, xbc_a_ref, dtraw_a_ref, xkeep_a_ref, proj_b_ref, xbc_b_ref, dtraw_b_ref, xkeep_b_ref):
        @pl.when(t_prev == 0)
        def _start_of_sequence():
            state_ref[...] = jnp.zeros_like(state_ref)
            xbc_b_ref[:, 0:PAD_ROWS, :] = jnp.zeros((CONV_DIM // LANES, PAD_ROWS, LANES), F32)

        x = x_ref[0]
        xkeep_a_ref[...] = x
        h = x * lax.rsqrt(jnp.mean(x * x, axis=-1, keepdims=True) + EPS) * pre_w_ref[...]
        hb = h.astype(BF16)

        def proj_task(j):
            cols = slice(j * PROJ_COL_BLOCK, (j + 1) * PROJ_COL_BLOCK)
            d = jnp.dot(hb, w_main_ref[:, cols], preferred_element_type=F32)
            if cols.start < OFF_XBC:
                proj_a_ref[:, cols] = d
            else:
                for i in range(PROJ_COL_BLOCK // LANES):
                    blk = (cols.start - OFF_XBC) // LANES + i
                    xbc_a_ref[blk, PAD_ROWS:PAD_ROWS + row_tile, :] = d[:, i * LANES:(i + 1) * LANES]

        def dt_task():
            dtraw_a_ref[...] = jnp.dot(hb, w_dt_ref[...], preferred_element_type=F32)

        a_tasks = [dt_task] + [functools.partial(proj_task, j) for j in range(MAIN_COLS // PROJ_COL_BLOCK)]

        def emit_a(n=1):
            for _ in range(n):
                if a_tasks:
                    a_tasks.pop(0)()

        def out_task(kb):
            ks = slice(kb * OUT_K_BLOCK, (kb + 1) * OUT_K_BLOCK)
            d = jnp.dot(ycat_ref[:, ks], w_out_ref[ks, :], preferred_element_type=F32)
            if kb == 0:
                mixed_ref[...] = d
            else:
                mixed_ref[...] += d

        tril_bf = jnp.where(causal, 1.0, 0.0).astype(BF16)
        lane_head_in_group = lax.broadcasted_iota(jnp.int32, (CHUNK, GROUP_WIDTH), 1) // SSD_HEAD_DIM
        lane_piece = col_i // SSD_HEADS
        a_row = -jnp.exp(alog_ref[...])

        emit_a(2)
        vnb = []
        for c in range(n_chunks):
            rows = slice(c * CHUNK, (c + 1) * CHUNK)
            v = proj_b_ref[rows, OFF_V:OFF_V + GMLP_WIDTH]
            mu = jnp.mean(v, axis=-1, keepdims=True)
            vc = v - mu
            vn = vc * lax.rsqrt(jnp.mean(vc * vc, axis=-1, keepdims=True) + EPS)
            vnb.append((vn * vw_ref[...] + vb_ref[...]).astype(BF16))
            emit_a()
        heads_per_block = OUT_K_BLOCK // GMLP_HEAD_DIM
        for kb in range(GMLP_WIDTH // OUT_K_BLOCK):
            for c in range(n_chunks):
                rows = slice(c * CHUNK, (c + 1) * CHUNK)
                for hd in range(kb * heads_per_block, (kb + 1) * heads_per_block):
                    cs_ = slice(hd * GMLP_HEAD_DIM, (hd + 1) * GMLP_HEAD_DIM)
                    sg = jnp.dot(wsm_ref[hd], vnb[c][:, cs_], preferred_element_type=F32) + bsf_ref[:, cs_]
                    u = proj_b_ref[rows, OFF_U + hd * GMLP_HEAD_DIM:OFF_U + (hd + 1) * GMLP_HEAD_DIM]
                    zg = proj_b_ref[rows, OFF_ZG + hd * GMLP_HEAD_DIM:OFF_ZG + (hd + 1) * GMLP_HEAD_DIM]
                    y = u * sg * jax.nn.silu(zg)
                    yn = y * lax.rsqrt(jnp.mean(y * y, axis=-1, keepdims=True) + EPS) * gw_ref[:, cs_]
                    ycat_ref[rows, cs_] = yn.astype(BF16)
            out_task(kb)
            emit_a()

        groups_per_block = OUT_K_BLOCK // GROUP_WIDTH
        for c in range(n_chunks):
            rows = slice(c * CHUNK, (c + 1) * CHUNK)
            r0 = PAD_ROWS + c * CHUNK

            for j in range(CONV_DIM // LANES):
                cc = slice(j * LANES, (j + 1) * LANES)
                acc = cb_ref[:, cc] + cw_ref[3:4, cc] * xbc_b_ref[j, r0:r0 + CHUNK, :]
                acc = acc + cw_ref[2:3, cc] * xbc_b_ref[j, r0 - 1:r0 - 1 + CHUNK, :]
                acc = acc + cw_ref[1:2, cc] * xbc_b_ref[j, r0 - 2:r0 - 2 + CHUNK, :]
                acc = acc + cw_ref[0:1, cc] * xbc_b_ref[j, r0 - 3:r0 - 3 + CHUNK, :]
                act_ref[:, cc] = jax.nn.silu(acc)
                if j % 4 == 3:
                    emit_a()

            dt = jax.nn.softplus(dtraw_b_ref[rows, :] + dtb_ref[...])
            da = dt * a_row
            da_pieces = jnp.concatenate(_split3(da), axis=1).astype(BF16)
            cs3 = jnp.dot(tril_bf, da_pieces, preferred_element_type=F32)
            cs = cs3[:, 0:LANES] + cs3[:, LANES:2 * LANES] + cs3[:, 2 * LANES:3 * LANES]
            cs_t = cs.T
            dt_t = dt.T
            cs_last = cs[CHUNK - 1:CHUNK, :]
            ecs = jnp.exp(cs)
            dd = dt * jnp.exp(cs_last - cs)
            pieces = _split3(dd) + _split3(ecs)
            packed = jnp.zeros((CHUNK, LANES), F32)
            for k, piece in enumerate(pieces):
                packed = jnp.where(lane_piece == k, piece, packed)
            xp = jnp.dot(packed.astype(BF16), expand_ref[...], preferred_element_type=F32)
            dd_x = xp[:, 0:SSD_WIDTH]
            ecs_x = xp[:, SSD_WIDTH:2 * SSD_WIDTH]
            cdec_x = ecs_x[CHUNK - 1:CHUNK, :]
            emit_a()

            for g in range(SSD_GROUPS):
                gc = slice(g * GROUP_WIDTH, (g + 1) * GROUP_WIDTH)
                xs_g = act_ref[:, gc]
                xs_gb = xs_g.astype(BF16)
                b_g = act_ref[:, OFF_B + g * D_STATE:OFF_B + (g + 1) * D_STATE].astype(BF16)
                c_g = act_ref[:, OFF_C + g * D_STATE:OFF_C + (g + 1) * D_STATE].astype(BF16)
                cb_g = lax.dot_general(c_g, b_g, (((1,), (1,)), ((), ())), preferred_element_type=F32)
                g_blocks = []
                x_blocks = []
                for r in range(HEADS_PER_GROUP):
                    hh = g * HEADS_PER_GROUP + r
                    seg = cs[:, hh:hh + 1] - cs_t[hh:hh + 1, :]
                    dec = jnp.exp(jnp.where(causal, seg, -jnp.inf))
                    g_blocks.append((cb_g * dec * dt_t[hh:hh + 1, :]).astype(BF16))
                    x_blocks.append(jnp.where(lane_head_in_group == r, xs_gb, jnp.zeros_like(xs_gb)))
                y_diag = jnp.dot(jnp.concatenate(g_blocks, axis=1), jnp.concatenate(x_blocks, axis=0),
                                 preferred_element_type=F32)
                st = state_ref[g]
                y_off = jnp.dot(c_g, st.astype(BF16), preferred_element_type=F32) * ecs_x[:, gc]
                xdd_g = (xs_g * dd_x[:, gc]).astype(BF16)
                st_new = lax.dot_general(b_g, xdd_g, (((0,), (0,)), ((), ())), preferred_element_type=F32)
                state_ref[g] = st * cdec_x[:, gc] + st_new
                y = y_diag + y_off + xs_g * dsk_ref[:, gc]
                z = proj_b_ref[rows, OFF_Z + g * GROUP_WIDTH:OFF_Z + (g + 1) * GROUP_WIDTH]
                yz = y * jax.nn.silu(z)
                yn = yz * lax.rsqrt(jnp.mean(yz * yz, axis=-1, keepdims=True) + EPS) * nw_ref[:, gc]
                ycat_ref[rows, GMLP_WIDTH + g * GROUP_WIDTH:GMLP_WIDTH + (g + 1) * GROUP_WIDTH] = yn.astype(BF16)
                if c == n_chunks - 1 and g % groups_per_block == groups_per_block - 1:
                    out_task(GMLP_WIDTH // OUT_K_BLOCK + g // groups_per_block)
                emit_a()

        emit_a(len(a_tasks))

        xbc_a_ref[:, 0:PAD_ROWS, :] = xbc_b_ref[:, row_tile:row_tile + PAD_ROWS, :]

        mx = mixed_ref[...]
        mn = mx * lax.rsqrt(jnp.mean(mx * mx, axis=-1, keepdims=True) + EPS) * post_w_ref[...]
        out_ref[0] = xkeep_b_ref[...] + mn

    @pl.when(s % 2 == 0)
    def _even_step():
        run(proj0_ref, xbc0_ref, dtraw0_ref, xkeep0_ref, proj1_ref, xbc1_ref, dtraw1_ref, xkeep1_ref)

    @pl.when(s % 2 == 1)
    def _odd_step():
        run(proj1_ref, xbc1_ref, dtraw1_ref, xkeep1_ref, proj0_ref, xbc0_ref, dtraw0_ref, xkeep0_ref)


def _vmem_limit_bytes(row_tile):
    weights = (D_MODEL * MAIN_COLS + D_MODEL * LANES + MIX_WIDTH * D_MODEL) * 2
    small = (GMLP_HEADS * CHUNK * CHUNK + CHUNK * GMLP_WIDTH) * 4 + 64 * 1024
    io = 2 * 2 * row_tile * D_MODEL * 4
    per_tile = (row_tile * OFF_XBC + (PAD_ROWS + row_tile) * CONV_DIM + row_tile * LANES + row_tile * D_MODEL) * 4
    scratch = (2 * per_tile + CHUNK * CONV_DIM * 4 + row_tile * MIX_WIDTH * 2
               + SSD_GROUPS * D_STATE * GROUP_WIDTH * 4 + GMLP_HEADS * CHUNK * CHUNK * 2
               + LANES * 2 * SSD_WIDTH * 2 + row_tile * D_MODEL * 4)
    staging = 2 * 2 * D_MODEL * W_CHUNK * 4
    temporaries = 8 * 1024 * 1024
    return weights + small + io + scratch + staging + temporaries


def _layer(layer, x, pre_w, w_main, w_dt, vw, vb, ws, bsf, gw, cw, cb, dtb, alog, dsk, nw, w_out, post_w,
           *, row_tile=ROW_TILE):
    bsz, seq, _ = x.shape
    assert seq % row_tile == 0 and row_tile % CHUNK == 0
    tiles_per_seq = seq // row_tile
    n_tiles = bsz * tiles_per_seq

    def const(shape):
        nd = len(shape)
        return pl.BlockSpec(shape, lambda s: (0,) * nd, pipeline_mode=pl.Buffered(1))

    def of_layer(shape):
        return pl.BlockSpec((None,) + shape, lambda s: (layer, 0, 0), pipeline_mode=pl.Buffered(1))

    def tile_index(i):
        return (i // tiles_per_seq, i % tiles_per_seq, 0)

    x_spec = pl.BlockSpec((1, row_tile, D_MODEL), lambda s: tile_index(jnp.minimum(s, n_tiles - 1)))
    out_spec = pl.BlockSpec((1, row_tile, D_MODEL), lambda s: tile_index(jnp.maximum(s - 1, 0)))
    in_hbm = pl.BlockSpec(memory_space=pl.ANY)
    in_specs = [
        x_spec,
        const((1, D_MODEL)),
        in_hbm,
        of_layer((D_MODEL, LANES)),
        const((1, GMLP_WIDTH)),
        const((1, GMLP_WIDTH)),
        const((GMLP_HEADS, CHUNK, CHUNK)),
        const((CHUNK, GMLP_WIDTH)),
        const((1, GMLP_WIDTH)),
        const((CONV_WIDTH, CONV_DIM)),
        const((1, CONV_DIM)),
        const((1, LANES)),
        const((1, LANES)),
        const((1, SSD_WIDTH)),
        const((1, SSD_WIDTH)),
        in_hbm,
        const((1, D_MODEL)),
    ]
    proj_buf = pltpu.VMEM((row_tile, OFF_XBC), F32)
    xbc_buf = pltpu.VMEM((CONV_DIM // LANES, PAD_ROWS + row_tile, LANES), F32)
    dtraw_buf = pltpu.VMEM((row_tile, LANES), F32)
    xkeep_buf = pltpu.VMEM((row_tile, D_MODEL), F32)
    scratch = [
        proj_buf, proj_buf, xbc_buf, xbc_buf, dtraw_buf, dtraw_buf, xkeep_buf, xkeep_buf,
        pltpu.VMEM((CHUNK, CONV_DIM), F32),
        pltpu.VMEM((row_tile, MIX_WIDTH), BF16),
        pltpu.VMEM((SSD_GROUPS, D_STATE, GROUP_WIDTH), F32),
        pltpu.VMEM((GMLP_HEADS, CHUNK, CHUNK), BF16),
        pltpu.VMEM((LANES, 2 * SSD_WIDTH), BF16),
        pltpu.VMEM((row_tile, D_MODEL), F32),
        pltpu.VMEM((D_MODEL, MAIN_COLS), BF16),
        pltpu.VMEM((MIX_WIDTH, D_MODEL), BF16),
        pltpu.VMEM((2, D_MODEL, W_CHUNK), F32),
        pltpu.VMEM((2, W_CHUNK, D_MODEL), F32),
        pltpu.SemaphoreType.DMA((2,)),
        pltpu.SemaphoreType.DMA((2,)),
    ]
    return pl.pallas_call(
        functools.partial(_layer_kernel, layer=layer, row_tile=row_tile, tiles_per_seq=tiles_per_seq),
        grid=(n_tiles + 1,),
        in_specs=in_specs,
        out_specs=out_spec,
        out_shape=jax.ShapeDtypeStruct(x.shape, x.dtype),
        scratch_shapes=scratch,
        compiler_params=pltpu.CompilerParams(
            dimension_semantics=("arbitrary",),
            vmem_limit_bytes=_vmem_limit_bytes(row_tile)),
        name="hybrid_layer",
    )(x, pre_w, w_main, w_dt, vw, vb, ws, bsf, gw, cw, cb, dtb, alog, dsk, nw, w_out, post_w)


def kernel(x, pre_norm_w, w_in, gmlp_v_norm_w, gmlp_v_norm_b, gmlp_ws, gmlp_bs, gmlp_norm_w,
           conv_w, conv_b, dt_bias, a_log, d_skip, ssd_norm_w, w_out, post_norm_w):
    depth = w_in.shape[0]
    w_dt = jnp.tile(w_in[:, :, MAIN_COLS:], (1, 1, HEAD_REPLICAS)).astype(BF16)
    h = x
    for i in range(depth):
        bsf = jnp.repeat(gmlp_bs[i].T, GMLP_HEAD_DIM, axis=1)
        h = _layer(
            i,
            h,
            pre_norm_w[i][None, :],
            w_in,
            w_dt,
            gmlp_v_norm_w[i][None, :],
            gmlp_v_norm_b[i][None, :],
            gmlp_ws[i],
            bsf,
            gmlp_norm_w[i][None, :],
            conv_w[i],
            conv_b[i][None, :],
            jnp.tile(dt_bias[i], HEAD_REPLICAS)[None, :],
            jnp.tile(a_log[i], HEAD_REPLICAS)[None, :],
            jnp.repeat(d_skip[i], SSD_HEAD_DIM)[None, :],
            ssd_norm_w[i][None, :],
            w_out,
            post_norm_w[i][None, :],
        )
    return h
```

```python
import functools

import jax
import jax.numpy as jnp
from jax import lax
from jax.experimental import pallas as pl
from jax.experimental.pallas import tpu as pltpu

D_MODEL = 1024
CHUNK = 128
GMLP_WIDTH = 1024
GMLP_HEADS = 8
GMLP_HEAD_DIM = 128
SSD_WIDTH = 1024
SSD_HEAD_DIM = 64
SSD_HEADS = 16
SSD_GROUPS = 4
HEADS_PER_GROUP = 4
GROUP_WIDTH = SSD_WIDTH // SSD_GROUPS
D_STATE = 128
CONV_WIDTH = 4
CONV_DIM = SSD_WIDTH + 2 * SSD_GROUPS * D_STATE
MIX_WIDTH = GMLP_WIDTH + SSD_WIDTH
EPS = 1e-6

OFF_U = 0
OFF_V = GMLP_WIDTH
OFF_ZG = 2 * GMLP_WIDTH
OFF_Z = 3 * GMLP_WIDTH
OFF_XBC = 3 * GMLP_WIDTH + SSD_WIDTH
MAIN_COLS = OFF_XBC + CONV_DIM
OFF_B = SSD_WIDTH
OFF_C = SSD_WIDTH + SSD_GROUPS * D_STATE

LANES = 128
SUBLANES = 8
MXU_DIM = 256
HEAD_REPLICAS = LANES // SSD_HEADS
PAD_ROWS = SUBLANES
ROW_TILE = 256
PROJ_COL_BLOCK = MXU_DIM
OUT_K_BLOCK = 2 * MXU_DIM
N_SPLIT = 3

F32 = jnp.float32
BF16 = jnp.bfloat16


def _split3(x):
    hi = x.astype(BF16).astype(F32)
    r1 = x - hi
    mid = r1.astype(BF16).astype(F32)
    lo = r1 - mid
    return hi, mid, lo


def _layer_kernel(x_ref, pre_w_ref, w_main_ref, w_dt_ref, vw_ref, vb_ref, ws_ref, bsf_ref,
                  gw_ref, cw_ref, cb_ref, dtb_ref, alog_ref, dsk_ref, nw_ref, w_out_ref,
                  post_w_ref, out_ref,
                  proj0_ref, proj1_ref, xbc0_ref, xbc1_ref, dtraw0_ref, dtraw1_ref, xkeep0_ref, xkeep1_ref,
                  act_ref, ycat_ref, state_ref, wsm_ref, expand_ref, mixed_ref,
                  *, row_tile, tiles_per_seq):
    n_chunks = row_tile // CHUNK
    s = pl.program_id(0)
    t_prev = (s + tiles_per_seq - 1) % tiles_per_seq

    row_i = lax.broadcasted_iota(jnp.int32, (CHUNK, CHUNK), 0)
    col_i = lax.broadcasted_iota(jnp.int32, (CHUNK, CHUNK), 1)
    causal = row_i >= col_i

    @pl.when(s == 0)
    def _first_step():
        proj1_ref[...] = jnp.zeros_like(proj1_ref)
        xbc1_ref[...] = jnp.zeros_like(xbc1_ref)
        dtraw1_ref[...] = jnp.zeros_like(dtraw1_ref)
        xkeep1_ref[...] = jnp.zeros_like(xkeep1_ref)
        state_ref[...] = jnp.zeros_like(state_ref)
        for hd in range(GMLP_HEADS):
            wsm_ref[hd] = jnp.where(causal, ws_ref[hd], 0.0).astype(BF16)
        ei = lax.broadcasted_iota(jnp.int32, (LANES, 2 * SSD_WIDTH), 0)
        ej = lax.broadcasted_iota(jnp.int32, (LANES, 2 * SSD_WIDTH), 1)
        which = ej // SSD_WIDTH
        head = (ej % SSD_WIDTH) // SSD_HEAD_DIM
        grp = ei // SSD_HEADS
        hit = (ei % SSD_HEADS == head) & (grp >= which * N_SPLIT) & (grp < (which + 1) * N_SPLIT)
        expand_ref[...] = jnp.where(hit, 1.0, 0.0).astype(BF16)

    def run(proj_a_ref, xbc_a_ref, dtraw_a_ref, xkeep_a_ref, proj_b_ref, xbc_b_ref, dtraw_b_ref, xkeep_b_ref):
        @pl.when(t_prev == 0)
        def _start_of_sequence():
            state_ref[...] = jnp.zeros_like(state_ref)
            xbc_b_ref[0:PAD_ROWS, :] = jnp.zeros((PAD_ROWS, CONV_DIM), F32)

        x = x_ref[0]
        xkeep_a_ref[...] = x
        h = x * lax.rsqrt(jnp.mean(x * x, axis=-1, keepdims=True) + EPS) * pre_w_ref[...]
        hb = h.astype(BF16)

        def proj_task(j):
            cols = slice(j * PROJ_COL_BLOCK, (j + 1) * PROJ_COL_BLOCK)
            d = jnp.dot(hb, w_main_ref[:, cols], preferred_element_type=F32)
            if cols.start < OFF_XBC:
                proj_a_ref[:, cols] = d
            else:
                xbc_a_ref[PAD_ROWS:PAD_ROWS + row_tile, cols.start - OFF_XBC:cols.stop - OFF_XBC] = d

        def dt_task():
            dtraw_a_ref[...] = jnp.dot(hb, w_dt_ref[...], preferred_element_type=F32)

        a_tasks = [dt_task] + [functools.partial(proj_task, j) for j in range(MAIN_COLS // PROJ_COL_BLOCK)]

        def emit_a(n=1):
            for _ in range(n):
                if a_tasks:
                    a_tasks.pop(0)()

        def out_task(kb):
            ks = slice(kb * OUT_K_BLOCK, (kb + 1) * OUT_K_BLOCK)
            d = jnp.dot(ycat_ref[:, ks], w_out_ref[ks, :], preferred_element_type=F32)
            if kb == 0:
                mixed_ref[...] = d
            else:
                mixed_ref[...] += d

        tril_bf = jnp.where(causal, 1.0, 0.0).astype(BF16)
        lane_head_in_group = lax.broadcasted_iota(jnp.int32, (CHUNK, GROUP_WIDTH), 1) // SSD_HEAD_DIM
        lane_piece = col_i // SSD_HEADS
        a_row = -jnp.exp(alog_ref[...])

        emit_a(2)
        vnb = []
        for c in range(n_chunks):
            rows = slice(c * CHUNK, (c + 1) * CHUNK)
            v = proj_b_ref[rows, OFF_V:OFF_V + GMLP_WIDTH]
            mu = jnp.mean(v, axis=-1, keepdims=True)
            vc = v - mu
            vn = vc * lax.rsqrt(jnp.mean(vc * vc, axis=-1, keepdims=True) + EPS)
            vnb.append((vn * vw_ref[...] + vb_ref[...]).astype(BF16))
            emit_a()
        heads_per_block = OUT_K_BLOCK // GMLP_HEAD_DIM
        for kb in range(GMLP_WIDTH // OUT_K_BLOCK):
            for c in range(n_chunks):
                rows = slice(c * CHUNK, (c + 1) * CHUNK)
                for hd in range(kb * heads_per_block, (kb + 1) * heads_per_block):
                    cs_ = slice(hd * GMLP_HEAD_DIM, (hd + 1) * GMLP_HEAD_DIM)
                    sg = jnp.dot(wsm_ref[hd], vnb[c][:, cs_], preferred_element_type=F32) + bsf_ref[:, cs_]
                    u = proj_b_ref[rows, OFF_U + hd * GMLP_HEAD_DIM:OFF_U + (hd + 1) * GMLP_HEAD_DIM]
                    zg = proj_b_ref[rows, OFF_ZG + hd * GMLP_HEAD_DIM:OFF_ZG + (hd + 1) * GMLP_HEAD_DIM]
                    y = u * sg * jax.nn.silu(zg)
                    yn = y * lax.rsqrt(jnp.mean(y * y, axis=-1, keepdims=True) + EPS) * gw_ref[:, cs_]
                    ycat_ref[rows, cs_] = yn.astype(BF16)
            out_task(kb)
            emit_a()

        groups_per_block = OUT_K_BLOCK // GROUP_WIDTH
        for c in range(n_chunks):
            rows = slice(c * CHUNK, (c + 1) * CHUNK)
            r0 = PAD_ROWS + c * CHUNK

            for j in range(CONV_DIM // LANES):
                cc = slice(j * LANES, (j + 1) * LANES)
                acc = cb_ref[:, cc] + cw_ref[3:4, cc] * xbc_b_ref[r0:r0 + CHUNK, cc]
                acc = acc + cw_ref[2:3, cc] * xbc_b_ref[r0 - 1:r0 - 1 + CHUNK, cc]
                acc = acc + cw_ref[1:2, cc] * xbc_b_ref[r0 - 2:r0 - 2 + CHUNK, cc]
                acc = acc + cw_ref[0:1, cc] * xbc_b_ref[r0 - 3:r0 - 3 + CHUNK, cc]
                act_ref[:, cc] = jax.nn.silu(acc)
                if j % 4 == 3:
                    emit_a()

            dt = jax.nn.softplus(dtraw_b_ref[rows, :] + dtb_ref[...])
            da = dt * a_row
            da_pieces = jnp.concatenate(_split3(da), axis=1).astype(BF16)
            cs3 = jnp.dot(tril_bf, da_pieces, preferred_element_type=F32)
            cs = cs3[:, 0:LANES] + cs3[:, LANES:2 * LANES] + cs3[:, 2 * LANES:3 * LANES]
            cs_t = cs.T
            dt_t = dt.T
            cs_last = cs[CHUNK - 1:CHUNK, :]
            ecs = jnp.exp(cs)
            dd = dt * jnp.exp(cs_last - cs)
            pieces = _split3(dd) + _split3(ecs)
            packed = jnp.zeros((CHUNK, LANES), F32)
            for k, piece in enumerate(pieces):
                packed = jnp.where(lane_piece == k, piece, packed)
            xp = jnp.dot(packed.astype(BF16), expand_ref[...], preferred_element_type=F32)
            dd_x = xp[:, 0:SSD_WIDTH]
            ecs_x = xp[:, SSD_WIDTH:2 * SSD_WIDTH]
            cdec_x = ecs_x[CHUNK - 1:CHUNK, :]
            emit_a()

            for g in range(SSD_GROUPS):
                gc = slice(g * GROUP_WIDTH, (g + 1) * GROUP_WIDTH)
                xs_g = act_ref[:, gc]
                xs_gb = xs_g.astype(BF16)
                b_g = act_ref[:, OFF_B + g * D_STATE:OFF_B + (g + 1) * D_STATE].astype(BF16)
                c_g = act_ref[:, OFF_C + g * D_STATE:OFF_C + (g + 1) * D_STATE].astype(BF16)
                cb_g = lax.dot_general(c_g, b_g, (((1,), (1,)), ((), ())), preferred_element_type=F32)
                g_blocks = []
                x_blocks = []
                for r in range(HEADS_PER_GROUP):
                    hh = g * HEADS_PER_GROUP + r
                    seg = cs[:, hh:hh + 1] - cs_t[hh:hh + 1, :]
                    dec = jnp.exp(jnp.where(causal, seg, -jnp.inf))
                    g_blocks.append((cb_g * dec * dt_t[hh:hh + 1, :]).astype(BF16))
                    x_blocks.append(jnp.where(lane_head_in_group == r, xs_gb, jnp.zeros_like(xs_gb)))
                y_diag = jnp.dot(jnp.concatenate(g_blocks, axis=1), jnp.concatenate(x_blocks, axis=0),
                                 preferred_element_type=F32)
                st = state_ref[g]
                y_off = jnp.dot(c_g, st.astype(BF16), preferred_element_type=F32) * ecs_x[:, gc]
                xdd_g = (xs_g * dd_x[:, gc]).astype(BF16)
                st_new = lax.dot_general(b_g, xdd_g, (((0,), (0,)), ((), ())), preferred_element_type=F32)
                state_ref[g] = st * cdec_x[:, gc] + st_new
                y = y_diag + y_off + xs_g * dsk_ref[:, gc]
                z = proj_b_ref[rows, OFF_Z + g * GROUP_WIDTH:OFF_Z + (g + 1) * GROUP_WIDTH]
                yz = y * jax.nn.silu(z)
                yn = yz * lax.rsqrt(jnp.mean(yz * yz, axis=-1, keepdims=True) + EPS) * nw_ref[:, gc]
                ycat_ref[rows, GMLP_WIDTH + g * GROUP_WIDTH:GMLP_WIDTH + (g + 1) * GROUP_WIDTH] = yn.astype(BF16)
                if c == n_chunks - 1 and g % groups_per_block == groups_per_block - 1:
                    out_task(GMLP_WIDTH // OUT_K_BLOCK + g // groups_per_block)
                emit_a()

        emit_a(len(a_tasks))

        xbc_a_ref[0:PAD_ROWS, :] = xbc_b_ref[row_tile:row_tile + PAD_ROWS, :]

        mx = mixed_ref[...]
        mn = mx * lax.rsqrt(jnp.mean(mx * mx, axis=-1, keepdims=True) + EPS) * post_w_ref[...]
        out_ref[0] = xkeep_b_ref[...] + mn

    @pl.when(s % 2 == 0)
    def _even_step():
        run(proj0_ref, xbc0_ref, dtraw0_ref, xkeep0_ref, proj1_ref, xbc1_ref, dtraw1_ref, xkeep1_ref)

    @pl.when(s % 2 == 1)
    def _odd_step():
        run(proj1_ref, xbc1_ref, dtraw1_ref, xkeep1_ref, proj0_ref, xbc0_ref, dtraw0_ref, xkeep0_ref)


def _vmem_limit_bytes(row_tile):
    weights = (D_MODEL * MAIN_COLS + D_MODEL * LANES + MIX_WIDTH * D_MODEL) * 2
    small = (GMLP_HEADS * CHUNK * CHUNK + CHUNK * GMLP_WIDTH) * 4 + 64 * 1024
    io = 2 * 2 * row_tile * D_MODEL * 4
    per_tile = (row_tile * OFF_XBC + (PAD_ROWS + row_tile) * CONV_DIM + row_tile * LANES + row_tile * D_MODEL) * 4
    scratch = (2 * per_tile + CHUNK * CONV_DIM * 4 + row_tile * MIX_WIDTH * 2
               + SSD_GROUPS * D_STATE * GROUP_WIDTH * 4 + GMLP_HEADS * CHUNK * CHUNK * 2
               + LANES * 2 * SSD_WIDTH * 2 + row_tile * D_MODEL * 4)
    temporaries = 8 * 1024 * 1024
    return weights + small + io + scratch + temporaries


def _layer(layer, x, pre_w, w_main, w_dt, vw, vb, ws, bsf, gw, cw, cb, dtb, alog, dsk, nw, w_out, post_w,
           *, row_tile=ROW_TILE):
    bsz, seq, _ = x.shape
    assert seq % row_tile == 0 and row_tile % CHUNK == 0
    tiles_per_seq = seq // row_tile
    n_tiles = bsz * tiles_per_seq

    def const(shape):
        nd = len(shape)
        return pl.BlockSpec(shape, lambda s: (0,) * nd, pipeline_mode=pl.Buffered(1))

    def of_layer(shape):
        return pl.BlockSpec((None,) + shape, lambda s: (layer, 0, 0), pipeline_mode=pl.Buffered(1))

    def tile_index(i):
        return (i // tiles_per_seq, i % tiles_per_seq, 0)

    x_spec = pl.BlockSpec((1, row_tile, D_MODEL), lambda s: tile_index(jnp.minimum(s, n_tiles - 1)))
    out_spec = pl.BlockSpec((1, row_tile, D_MODEL), lambda s: tile_index(jnp.maximum(s - 1, 0)))
    in_specs = [
        x_spec,
        const((1, D_MODEL)),
        of_layer((D_MODEL, MAIN_COLS)),
        of_layer((D_MODEL, LANES)),
        const((1, GMLP_WIDTH)),
        const((1, GMLP_WIDTH)),
        const((GMLP_HEADS, CHUNK, CHUNK)),
        const((CHUNK, GMLP_WIDTH)),
        const((1, GMLP_WIDTH)),
        const((CONV_WIDTH, CONV_DIM)),
        const((1, CONV_DIM)),
        const((1, LANES)),
        const((1, LANES)),
        const((1, SSD_WIDTH)),
        const((1, SSD_WIDTH)),
        of_layer((MIX_WIDTH, D_MODEL)),
        const((1, D_MODEL)),
    ]
    proj_buf = pltpu.VMEM((row_tile, OFF_XBC), F32)
    xbc_buf = pltpu.VMEM((PAD_ROWS + row_tile, CONV_DIM), F32)
    dtraw_buf = pltpu.VMEM((row_tile, LANES), F32)
    xkeep_buf = pltpu.VMEM((row_tile, D_MODEL), F32)
    scratch = [
        proj_buf, proj_buf, xbc_buf, xbc_buf, dtraw_buf, dtraw_buf, xkeep_buf, xkeep_buf,
        pltpu.VMEM((CHUNK, CONV_DIM), F32),
        pltpu.VMEM((row_tile, MIX_WIDTH), BF16),
        pltpu.VMEM((SSD_GROUPS, D_STATE, GROUP_WIDTH), F32),
        pltpu.VMEM((GMLP_HEADS, CHUNK, CHUNK), BF16),
        pltpu.VMEM((LANES, 2 * SSD_WIDTH), BF16),
        pltpu.VMEM((row_tile, D_MODEL), F32),
    ]
    return pl.pallas_call(
        functools.partial(_layer_kernel, row_tile=row_tile, tiles_per_seq=tiles_per_seq),
        grid=(n_tiles + 1,),
        in_specs=in_specs,
        out_specs=out_spec,
        out_shape=jax.ShapeDtypeStruct(x.shape, x.dtype),
        scratch_shapes=scratch,
        compiler_params=pltpu.CompilerParams(
            dimension_semantics=("arbitrary",),
            vmem_limit_bytes=_vmem_limit_bytes(row_tile)),
        name="hybrid_layer",
    )(x, pre_w, w_main, w_dt, vw, vb, ws, bsf, gw, cw, cb, dtb, alog, dsk, nw, w_out, post_w)


def kernel(x, pre_norm_w, w_in, gmlp_v_norm_w, gmlp_v_norm_b, gmlp_ws, gmlp_bs, gmlp_norm_w,
           conv_w, conv_b, dt_bias, a_log, d_skip, ssd_norm_w, w_out, post_norm_w):
    depth = w_in.shape[0]
    w_main = w_in.astype(BF16)
    w_dt = jnp.tile(w_in[:, :, MAIN_COLS:], (1, 1, HEAD_REPLICAS)).astype(BF16)
    w_out_bf = w_out.astype(BF16)
    h = x
    for i in range(depth):
        bsf = jnp.repeat(gmlp_bs[i].T, GMLP_HEAD_DIM, axis=1)
        h = _layer(
            i,
            h,
            pre_norm_w[i][None, :],
            w_main,
            w_dt,
            gmlp_v_norm_w[i][None, :],
            gmlp_v_norm_b[i][None, :],
            gmlp_ws[i],
            bsf,
            gmlp_norm_w[i][None, :],
            conv_w[i],
            conv_b[i][None, :],
            jnp.tile(dt_bias[i], HEAD_REPLICAS)[None, :],
            jnp.tile(a_log[i], HEAD_REPLICAS)[None, :],
            jnp.repeat(d_skip[i], SSD_HEAD_DIM)[None, :],
            ssd_norm_w[i][None, :],
            w_out_bf,
            post_norm_w[i][None, :],
        )
    return h
```
